```python
import math
import jax, jax.numpy as jnp
from jax import lax
import numpy as np

D_MODEL = 1024
BATCH = 1
SEQ = 16384
DEPTH = 1
DEC_BATCH = 128
DEC_SEQ = 1
PAST_LEN = 8192
PAGE_SIZE = 128

HEAD_DIM = 64
SB_HEADS = 8
NSA_HEADS = 8
NSA_KV_HEADS = 2
NSA_GROUP = NSA_HEADS // NSA_KV_HEADS
SB_WIDTH = SB_HEADS * HEAD_DIM
NSA_WIDTH = NSA_HEADS * HEAD_DIM
KV_WIDTH = NSA_KV_HEADS * HEAD_DIM
CMP_BLOCK = 32
CMP_STRIDE = 16
CMP_HIDDEN = 256
SEL_BLOCK = 64
N_SELECT = 16
WINDOW = 512
Q_BLOCK = 128
N_BUCKETS = 32
MAX_DISTANCE = 128
N_GROUPS = 4
EXPERTS_PER_GROUP = 8
N_EXPERTS = N_GROUPS * EXPERTS_PER_GROUP
TOP_K_INNER = 2
EXPERT_FF = 256
PLE_DIM = 256
RMS_EPS = 1e-6
NEG_INF = -1e30
FORCE_BONUS = 1e4
IN_SIZES = (SB_WIDTH,) * 3 + (NSA_WIDTH,) + (KV_WIDTH,) * 6 + (3 * NSA_HEADS, D_MODEL, D_MODEL)
IN_COLS = sum(IN_SIZES)

kernel_name = 'stick_breaking_nsa_hier_moe_step'


def rmsnorm(x, g):
    x32 = x.astype(jnp.float32)
    y = x32 * lax.rsqrt(jnp.mean(x32 * x32, axis=-1, keepdims=True) + RMS_EPS) * g.astype(jnp.float32)
    return y.astype(x.dtype)


def masked_softmax(s, mask):
    s = jnp.where(mask, s, NEG_INF)
    m = jnp.max(s, axis=-1, keepdims=True)
    e = jnp.where(mask, jnp.exp(s - m), 0.0)
    return e / jnp.maximum(jnp.sum(e, axis=-1, keepdims=True), 1e-30)


def t5_bucket(dist):
    n = jnp.maximum(dist, 0)
    max_exact = N_BUCKETS // 2
    nf = jnp.maximum(n, 1).astype(jnp.float32)
    large = max_exact + (jnp.log(nf / max_exact) / math.log(MAX_DISTANCE / max_exact)
                         * (N_BUCKETS - max_exact)).astype(jnp.int32)
    large = jnp.minimum(large, N_BUCKETS - 1)
    return jnp.where(n < max_exact, n, large)


def split_in(u):
    offs = [int(o) for o in np.cumsum(IN_SIZES)[:-1]]
    (q_sb, k_sb, v_sb, q_nsa, k_c, v_c, k_s, v_s, k_w, v_w,
     g_nsa, gate_sb, gate_nsa) = jnp.split(u, offs, axis=-1)
    hd = lambda a, n: a.reshape(a.shape[:-1] + (n, HEAD_DIM))
    kv = lambda a: hd(a, NSA_KV_HEADS)
    g_nsa = jax.nn.sigmoid(g_nsa).reshape(g_nsa.shape[:-1] + (NSA_HEADS, 3))
    return (hd(q_sb, SB_HEADS), hd(k_sb, SB_HEADS), hd(v_sb, SB_HEADS), hd(q_nsa, NSA_HEADS),
            kv(k_c), kv(v_c), kv(k_s), kv(v_s), kv(k_w), kv(v_w), g_nsa,
            jax.nn.sigmoid(gate_sb), jax.nn.sigmoid(gate_nsa))


def sb_block(q, qpos, k, v, kpos):
    f32 = jnp.float32
    z = jnp.einsum('qhd,khd->hqk', q.astype(f32), k.astype(f32)) * HEAD_DIM ** -0.5
    mask = kpos[None, None, :] < qpos[None, :, None]
    log_keep = jnp.where(mask, jax.nn.log_sigmoid(-z), 0.0)
    after = lax.cumsum(log_keep, axis=2, reverse=True) - log_keep
    a = jnp.where(mask, jnp.exp(jax.nn.log_sigmoid(z) + after), 0.0)
    return jnp.einsum('hqk,khd->qhd', a, v.astype(f32))


def compress(rows, pe, w1, b1, w2):
    n_c = (rows.shape[0] - CMP_BLOCK) // CMP_STRIDE + 1
    starts = jnp.arange(n_c) * CMP_STRIDE
    idx = starts[:, None] + jnp.arange(CMP_BLOCK)[None, :]
    blk = rows[idx] + pe[None, :, None, :]
    flat = blk.transpose(0, 2, 1, 3).reshape(n_c, NSA_KV_HEADS, CMP_BLOCK * HEAD_DIM)
    hid = jax.nn.gelu(flat @ w1 + b1)
    return hid @ w2, starts + CMP_BLOCK - 1


def nsa_block(q, qpos, gates, cmp_k, cmp_v, cmp_end, sel_k, sel_v, win_k, win_v, win_pos, table):
    f32 = jnp.float32
    scale = HEAD_DIM ** -0.5
    tq = q.shape[0]
    qg = q.astype(f32).reshape(tq, NSA_KV_HEADS, NSA_GROUP, HEAD_DIM)
    tab = table.astype(f32).reshape(N_BUCKETS, NSA_KV_HEADS, NSA_GROUP)
    dist_c = qpos[:, None] - cmp_end[None, :]
    s_c = (jnp.einsum('qkgd,ckd->kgqc', qg, cmp_k.astype(f32)) * scale
           + tab[t5_bucket(dist_c)].transpose(2, 3, 0, 1))
    p_c = masked_softmax(s_c, (dist_c >= 0)[None, None])
    o_c = jnp.einsum('kgqc,ckd->qkgd', p_c, cmp_v.astype(f32))
    n_cmp, n_sel = cmp_k.shape[0], sel_k.shape[0]
    c_start = jnp.arange(n_cmp) * CMP_STRIDE
    blk = jnp.arange(n_sel)
    b_start = blk * SEL_BLOCK
    overlap = ((c_start[:, None] < b_start[None, :] + SEL_BLOCK)
               & (c_start[:, None] + CMP_BLOCK > b_start[None, :])).astype(f32)
    imp = jnp.einsum('kgqc,cj->kqj', p_c, overlap)
    q_blk = qpos // SEL_BLOCK
    valid = blk[None, :] <= q_blk[:, None]
    forced = (blk[None, :] == 0) | (blk[None, :] >= q_blk[:, None] - 1)
    score = jnp.where(valid[None], imp + jnp.where(forced, FORCE_BONUS, 0.0)[None], NEG_INF)
    n_top = min(N_SELECT, n_sel)
    _, sel_idx = lax.top_k(score, n_top)
    take = jax.vmap(lambda blocks, ix: blocks[ix])
    gk = take(sel_k.astype(f32).transpose(2, 0, 1, 3), sel_idx)
    gv = take(sel_v.astype(f32).transpose(2, 0, 1, 3), sel_idx)
    kpos_s = sel_idx[..., None] * SEL_BLOCK + jnp.arange(SEL_BLOCK)
    dist_s = qpos[None, :, None, None] - kpos_s
    bias_s = jax.vmap(lambda tb, dd: tb[t5_bucket(dd)], in_axes=(1, 0))(tab, dist_s)
    s_s = jnp.einsum('qkgd,kqnld->kgqnl', qg, gk) * scale + bias_s.transpose(0, 4, 1, 2, 3)
    n_keys = n_top * SEL_BLOCK
    p_s = masked_softmax(s_s.reshape(NSA_KV_HEADS, NSA_GROUP, tq, n_keys),
                         (dist_s >= 0).reshape(NSA_KV_HEADS, 1, tq, n_keys))
    o_s = jnp.einsum('kgqm,kqmd->qkgd', p_s, gv.reshape(NSA_KV_HEADS, tq, n_keys, HEAD_DIM))
    dist_w = qpos[:, None] - win_pos[None, :]
    mask_w = (dist_w >= 0) & (dist_w < WINDOW) & (win_pos[None, :] >= 0)
    s_w = (jnp.einsum('qkgd,wkd->kgqw', qg, win_k.astype(f32)) * scale
           + tab[t5_bucket(dist_w)].transpose(2, 3, 0, 1))
    p_w = masked_softmax(s_w, mask_w[None, None])
    o_w = jnp.einsum('kgqw,wkd->qkgd', p_w, win_v.astype(f32))
    g = gates.astype(f32).reshape(tq, NSA_KV_HEADS, NSA_GROUP, 3)
    o = g[..., 0:1] * o_c + g[..., 1:2] * o_s + g[..., 2:3] * o_w
    return o.reshape(tq, NSA_WIDTH)


def prompt_mix(q_sb, k_sb, v_sb, q_nsa, k_c, v_c, k_s, v_s, k_w, v_w, g_nsa, cmpk_p, cmpv_p, table):
    seq = q_sb.shape[0]
    kpos = jnp.arange(seq)
    cmp_k, cmp_end = compress(k_c, *cmpk_p)
    cmp_v, _ = compress(v_c, *cmpv_p)
    sel_k = k_s.reshape(seq // SEL_BLOCK, SEL_BLOCK, NSA_KV_HEADS, HEAD_DIM)
    sel_v = v_s.reshape(seq // SEL_BLOCK, SEL_BLOCK, NSA_KV_HEADS, HEAD_DIM)
    pad = ((WINDOW, 0), (0, 0), (0, 0))
    kw_pad, vw_pad = jnp.pad(k_w, pad), jnp.pad(v_w, pad)

    def block(i):
        start = i * Q_BLOCK
        qpos = start + jnp.arange(Q_BLOCK)
        sl = lambda a: lax.dynamic_slice_in_dim(a, start, Q_BLOCK, axis=0)
        wl = lambda a: lax.dynamic_slice_in_dim(a, start, WINDOW + Q_BLOCK, axis=0)
        wpos = start - WINDOW + jnp.arange(WINDOW + Q_BLOCK)
        o_sb = sb_block(sl(q_sb), qpos, k_sb, v_sb, kpos)
        o_nsa = nsa_block(sl(q_nsa), qpos, sl(g_nsa), cmp_k, cmp_v, cmp_end, sel_k, sel_v,
                          wl(kw_pad), wl(vw_pad), wpos, table)
        return o_sb.reshape(Q_BLOCK, SB_WIDTH), o_nsa

    o_sb, o_nsa = lax.map(block, jnp.arange(seq // Q_BLOCK))
    return o_sb.reshape(seq, SB_WIDTH), o_nsa.reshape(seq, NSA_WIDTH)


def sample_mix(args, caches, layer, cmpk_p, cmpv_p, table):
    (pages, q_sb, k_sb, v_sb, q_nsa, k_c, v_c, k_s, v_s, k_w, v_w, g_nsa, wb_k, wb_v) = args
    c_sbk, c_sbv, c_ck, c_cv, c_sk, c_sv = caches
    n_new = q_sb.shape[0]
    past_len = pages.shape[0] * PAGE_SIZE
    total = past_len + n_new

    def with_past(c, new):
        past = c[layer, pages]
        past = past.reshape((past_len,) + past.shape[2:])
        return jnp.concatenate([past.astype(new.dtype), new], axis=0)

    kpos = jnp.arange(total)
    qpos = past_len + jnp.arange(n_new)
    o_sb = sb_block(q_sb, qpos, with_past(c_sbk, k_sb), with_past(c_sbv, v_sb), kpos)
    cmp_k, cmp_end = compress(with_past(c_ck, k_c), *cmpk_p)
    cmp_v, _ = compress(with_past(c_cv, v_c), *cmpv_p)
    n_sel = -(-total // SEL_BLOCK)
    pad = n_sel * SEL_BLOCK - total

    def blocks(c, new):
        rows = jnp.pad(with_past(c, new), ((0, pad), (0, 0), (0, 0)))
        return rows.reshape(n_sel, SEL_BLOCK, NSA_KV_HEADS, HEAD_DIM)

    w_buf = wb_k.shape[0]
    win_k = jnp.concatenate([wb_k.astype(k_w.dtype), k_w], axis=0)
    win_v = jnp.concatenate([wb_v.astype(v_w.dtype), v_w], axis=0)
    wpos = past_len - w_buf + jnp.arange(w_buf + n_new)
    o_nsa = nsa_block(q_nsa, qpos, g_nsa, cmp_k, cmp_v, cmp_end, blocks(c_sk, k_s), blocks(c_sv, v_s),
                      win_k, win_v, wpos, table)
    return o_sb.reshape(n_new, SB_WIDTH), o_nsa


def hier_moe(x, w_rg, w_re, w_eg, w_eu, w_ed):
    t = x.shape[0]
    g_logits = (x @ w_rg).astype(jnp.float32)
    g_prob = jax.nn.softmax(g_logits, axis=-1)
    g_idx = jnp.argmax(g_logits, axis=-1)
    g_w = jnp.take_along_axis(g_prob, g_idx[:, None], axis=1)
    e_logits = (x @ w_re).astype(jnp.float32).reshape(t, N_GROUPS, EXPERTS_PER_GROUP)
    e_in = jnp.take_along_axis(e_logits, g_idx[:, None, None], axis=1)[:, 0]
    top_p, top_i = lax.top_k(jax.nn.softmax(e_in, axis=-1), TOP_K_INNER)
    top_p = top_p / jnp.sum(top_p, axis=-1, keepdims=True)
    eid = g_idx[:, None] * EXPERTS_PER_GROUP + top_i
    combine = jnp.sum(jax.nn.one_hot(eid, N_EXPERTS, dtype=jnp.float32)
                      * (g_w * top_p)[..., None], axis=1)
    hid = jax.nn.silu(jnp.einsum('td,edf->tef', x, w_eg)) * jnp.einsum('td,edf->tef', x, w_eu)
    return jnp.einsum('tef,efd->td', hid * combine[..., None].astype(hid.dtype), w_ed)


def layer_out(h, o_sb, o_nsa, gate_sb, gate_nsa, p_emb, w_b_sb, w_b_nsa, w_o, g_ffn,
              w_rg, w_re, w_eg, w_eu, w_ed, g_ple, w_pg, w_pe):
    merged = gate_sb * (o_sb.astype(h.dtype) @ w_b_sb) + gate_nsa * (o_nsa.astype(h.dtype) @ w_b_nsa)
    h = h + merged @ w_o
    hn = rmsnorm(h, g_ffn)
    h = h + hier_moe(hn.reshape(-1, D_MODEL), w_rg, w_re, w_eg, w_eu, w_ed).reshape(h.shape)
    gate = jax.nn.sigmoid(rmsnorm(h, g_ple) @ w_pg)
    return h + gate * (p_emb @ w_pe)


def setup_inputs(seed: int = 0) -> dict:
    key = jax.random.key(seed)
    ks = iter(jax.random.split(key, 48))
    f32 = jnp.float32
    n_pages = PAST_LEN // PAGE_SIZE
    n_used = DEC_BATCH * n_pages
    n_pool = n_used + max(1, n_used // 4)
    w_buf = min(WINDOW, PAST_LEN)

    def nrm(shape, scale=1.0):
        return jax.random.normal(next(ks), shape, f32) * scale

    def gain(shape):
        return 1.0 + nrm(shape, 0.01)

    page_table = jax.random.permutation(next(ks), n_pool)[:n_used].reshape(DEC_BATCH, n_pages).astype(jnp.int32)
    return {
        'x_prompt': nrm((BATCH, SEQ, D_MODEL)),
        'x_sample': nrm((DEC_BATCH, DEC_SEQ, D_MODEL)),
        'cache_sb_k': nrm((DEPTH, n_pool, PAGE_SIZE, SB_HEADS, HEAD_DIM)),
        'cache_sb_v': nrm((DEPTH, n_pool, PAGE_SIZE, SB_HEADS, HEAD_DIM)),
        'cache_cmp_k': nrm((DEPTH, n_pool, PAGE_SIZE, NSA_KV_HEADS, HEAD_DIM)),
        'cache_cmp_v': nrm((DEPTH, n_pool, PAGE_SIZE, NSA_KV_HEADS, HEAD_DIM)),
        'cache_sel_k': nrm((DEPTH, n_pool, PAGE_SIZE, NSA_KV_HEADS, HEAD_DIM)),
        'cache_sel_v': nrm((DEPTH, n_pool, PAGE_SIZE, NSA_KV_HEADS, HEAD_DIM)),
        'state_win_k': nrm((DEPTH, DEC_BATCH, w_buf, NSA_KV_HEADS, HEAD_DIM)),
        'state_win_v': nrm((DEPTH, DEC_BATCH, w_buf, NSA_KV_HEADS, HEAD_DIM)),
        'page_table': page_table,
        'p_prompt': nrm((DEPTH, BATCH, SEQ, PLE_DIM)),
        'p_sample': nrm((DEPTH, DEC_BATCH, DEC_SEQ, PLE_DIM)),
        'norm_mix': gain((DEPTH, D_MODEL)),
        'w_in': nrm((DEPTH, D_MODEL, IN_COLS), D_MODEL ** -0.5),
        'cmp_pe_k': nrm((DEPTH, CMP_BLOCK, HEAD_DIM), 0.1),
        'cmp_w1_k': nrm((DEPTH, CMP_BLOCK * HEAD_DIM, CMP_HIDDEN), (CMP_BLOCK * HEAD_DIM) ** -0.5),
        'cmp_b1_k': nrm((DEPTH, CMP_HIDDEN), 0.01),
        'cmp_w2_k': nrm((DEPTH, CMP_HIDDEN, HEAD_DIM), CMP_HIDDEN ** -0.5),
        'cmp_pe_v': nrm((DEPTH, CMP_BLOCK, HEAD_DIM), 0.1),
        'cmp_w1_v': nrm((DEPTH, CMP_BLOCK * HEAD_DIM, CMP_HIDDEN), (CMP_BLOCK * HEAD_DIM) ** -0.5),
        'cmp_b1_v': nrm((DEPTH, CMP_HIDDEN), 0.01),
        'cmp_w2_v': nrm((DEPTH, CMP_HIDDEN, HEAD_DIM), CMP_HIDDEN ** -0.5),
        'rel_bias_table': nrm((N_BUCKETS, NSA_HEADS), 0.1),
        'w_branch_sb': nrm((DEPTH, SB_WIDTH, D_MODEL), SB_WIDTH ** -0.5),
        'w_branch_nsa': nrm((DEPTH, NSA_WIDTH, D_MODEL), NSA_WIDTH ** -0.5),
        'w_out': nrm((DEPTH, D_MODEL, D_MODEL), D_MODEL ** -0.5),
        'norm_ffn': gain((DEPTH, D_MODEL)),
        'w_router_group': nrm((DEPTH, D_MODEL, N_GROUPS), D_MODEL ** -0.5),
        'w_router_expert': nrm((DEPTH, D_MODEL, N_EXPERTS), D_MODEL ** -0.5),
        'w_exp_gate': nrm((DEPTH, N_EXPERTS, D_MODEL, EXPERT_FF), D_MODEL ** -0.5),
        'w_exp_up': nrm((DEPTH, N_EXPERTS, D_MODEL, EXPERT_FF), D_MODEL ** -0.5),
        'w_exp_down': nrm((DEPTH, N_EXPERTS, EXPERT_FF, D_MODEL), EXPERT_FF ** -0.5),
        'norm_ple': gain((DEPTH, D_MODEL)),
        'w_ple_gate': nrm((DEPTH, D_MODEL, D_MODEL), D_MODEL ** -0.5),
        'w_ple': nrm((DEPTH, PLE_DIM, D_MODEL), PLE_DIM ** -0.5),
        'norm_final': gain((D_MODEL,)),
    }


def reference(x_prompt, x_sample, cache_sb_k, cache_sb_v, cache_cmp_k, cache_cmp_v, cache_sel_k,
              cache_sel_v, state_win_k, state_win_v, page_table, p_prompt, p_sample, norm_mix, w_in,
              cmp_pe_k, cmp_w1_k, cmp_b1_k, cmp_w2_k, cmp_pe_v, cmp_w1_v, cmp_b1_v, cmp_w2_v,
              rel_bias_table, w_branch_sb, w_branch_nsa, w_out, norm_ffn, w_router_group,
              w_router_expert, w_exp_gate, w_exp_up, w_exp_down, norm_ple, w_ple_gate, w_ple,
              norm_final):
    caches = (cache_sb_k, cache_sb_v, cache_cmp_k, cache_cmp_v, cache_sel_k, cache_sel_v)
    h_p, h_s = x_prompt, x_sample
    per_layer = []
    for l in range(DEPTH):
        cmpk_p = (cmp_pe_k[l], cmp_w1_k[l], cmp_b1_k[l], cmp_w2_k[l])
        cmpv_p = (cmp_pe_v[l], cmp_w1_v[l], cmp_b1_v[l], cmp_w2_v[l])
        lw = (w_branch_sb[l], w_branch_nsa[l], w_out[l], norm_ffn[l], w_router_group[l],
              w_router_expert[l], w_exp_gate[l], w_exp_up[l], w_exp_down[l], norm_ple[l],
              w_ple_gate[l], w_ple[l])
        pp = split_in(rmsnorm(h_p, norm_mix[l]) @ w_in[l])
        o_sb_p, o_nsa_p = jax.vmap(prompt_mix, in_axes=(0,) * 11 + (None, None, None))(
            *pp[:11], cmpk_p, cmpv_p, rel_bias_table)
        h_p = layer_out(h_p, o_sb_p, o_nsa_p, pp[11], pp[12], p_prompt[l], *lw)
        seq_p = pp[8].shape[1]
        win_k_p = pp[8][:, seq_p - min(WINDOW, seq_p):]
        win_v_p = pp[9][:, seq_p - min(WINDOW, seq_p):]
        ps = split_in(rmsnorm(h_s, norm_mix[l]) @ w_in[l])
        seq_args = (page_table,) + tuple(ps[:11]) + (state_win_k[l], state_win_v[l])
        o_sb_s, o_nsa_s = lax.map(
            lambda a: sample_mix(a, caches, l, cmpk_p, cmpv_p, rel_bias_table), seq_args)
        h_s = layer_out(h_s, o_sb_s, o_nsa_s, ps[11], ps[12], p_sample[l], *lw)
        n_new = ps[8].shape[1]
        win_k_s = jnp.concatenate([state_win_k[l].astype(ps[8].dtype), ps[8]], axis=1)[:, n_new:]
        win_v_s = jnp.concatenate([state_win_v[l].astype(ps[9].dtype), ps[9]], axis=1)[:, n_new:]
        per_layer.append((pp[1], pp[2], pp[4], pp[5], pp[6], pp[7], win_k_p, win_v_p,
                          ps[1], ps[2], ps[4], ps[5], ps[6], ps[7], win_k_s, win_v_s))
    stacked = [jnp.stack(z, axis=0) for z in zip(*per_layer)]
    (sb_k_p, sb_v_p, cmp_k_p, cmp_v_p, sel_k_p, sel_v_p, win_k_p, win_v_p,
     sb_k_s, sb_v_s, cmp_k_s, cmp_v_s, sel_k_s, sel_v_s, win_k_s, win_v_s) = stacked
    y_prompt = rmsnorm(h_p, norm_final)
    y_sample = rmsnorm(h_s, norm_final)
    return (y_prompt, y_sample, sb_k_p, sb_v_p, cmp_k_p, cmp_v_p, sel_k_p, sel_v_p, win_k_p, win_v_p,
            sb_k_s, sb_v_s, cmp_k_s, cmp_v_s, sel_k_s, sel_v_s, win_k_s, win_v_s)
```

```python
import functools
import math

import numpy as np
import jax
import jax.numpy as jnp
from jax import lax
from jax.experimental import pallas as pl
from jax.experimental.pallas import tpu as pltpu

F32 = jnp.float32
BF16 = jnp.bfloat16
HIGHEST = lax.Precision.HIGHEST

D_MODEL = 1024
HEAD_DIM = 64
SB_HEADS = 8
NSA_HEADS = 8
NSA_KV = 2
NSA_GROUP = NSA_HEADS // NSA_KV
SB_WIDTH = SB_HEADS * HEAD_DIM
NSA_WIDTH = NSA_HEADS * HEAD_DIM
KV_WIDTH = NSA_KV * HEAD_DIM
CMP_BLOCK = 32
CMP_STRIDE = 16
CMP_HIDDEN = 256
SEL_BLOCK = 64
N_SELECT = 16
N_FORCED = 3
WINDOW = 512
N_BUCKETS = 32
MAX_DISTANCE = 128
N_GROUPS = 4
EXPERTS_PER_GROUP = 8
N_EXPERTS = N_GROUPS * EXPERTS_PER_GROUP
EXPERT_FF = 256
PLE_DIM = 256
PAGE = 128
RMS_EPS = 1e-6
NEG = -1e30
LANES = 128
EXP_UNDERFLOW = -104.0
VMEM_LIMIT = 56 * 1024 * 1024

SEL_PAD = 3 * LANES
WIN_PAD = WINDOW


def _cparams(sem):
    return pltpu.CompilerParams(dimension_semantics=sem, vmem_limit_bytes=VMEM_LIMIT)


def _rms(x, g):
    return x * lax.rsqrt(jnp.mean(x * x, axis=-1, keepdims=True) + RMS_EPS) * g


def _dot(a, b):
    return jnp.dot(a, b, preferred_element_type=F32)


def _dot_hp(a, b):
    return jnp.dot(a, b, preferred_element_type=F32, precision=HIGHEST)


def _dot_nt(a, b):
    return lax.dot_general(a, b, (((1,), (1,)), ((), ())), preferred_element_type=F32)


def _dot_tn_hp(a, b):
    return lax.dot_general(a, b, (((0,), (0,)), ((), ())), preferred_element_type=F32,
                           precision=HIGHEST)


def _r16(a):
    return a.astype(BF16).astype(F32)


def _t5_bucket_np(n):
    n = np.maximum(n, 0)
    max_exact = N_BUCKETS // 2
    nf = np.maximum(n, 1).astype(np.float32)
    large = max_exact + (np.log(nf / np.float32(max_exact)) / np.float32(math.log(MAX_DISTANCE / max_exact))
                         * np.float32(N_BUCKETS - max_exact)).astype(np.int32)
    large = np.minimum(large, N_BUCKETS - 1)
    return np.where(n < max_exact, n, large)


def _proj_kernel(x_ref, g_ref, w_ref, wt_ref,
                 qsb_ref, ksb_ref, vsb_ref, vsbb_ref, qn_ref,
                 kc_ref, vc_ref, ks_ref, vs_ref, kw_ref, vw_ref,
                 vsa_ref, vwa_ref, ksbt_ref, kst_ref, kwt_ref):
    xb = _rms(x_ref[...], g_ref[...]).astype(BF16)

    def cols(a, b):
        return _dot(xb, w_ref[:, a:b])

    u = cols(0, 512)
    for h in range(SB_HEADS):
        qsb_ref[h] = u[:, h * HEAD_DIM:(h + 1) * HEAD_DIM].astype(BF16)
    ksb_ref[...] = cols(512, 1024)
    v = cols(1024, 1536)
    vsb_ref[...] = v
    vsbb_ref[...] = v.astype(BF16)
    u = cols(1536, 2048)
    for h in range(NSA_HEADS):
        qn_ref[h] = u[:, h * HEAD_DIM:(h + 1) * HEAD_DIM].astype(BF16)
    for i, r in enumerate((kc_ref, vc_ref, ks_ref, vs_ref, kw_ref, vw_ref)):
        r[...] = cols(2048 + i * KV_WIDTH, 2048 + (i + 1) * KV_WIDTH)
    ones_col = lax.broadcasted_iota(jnp.int32, (xb.shape[0], LANES), 1) == HEAD_DIM
    base = 2048 + 6 * KV_WIDTH
    for kv in range(NSA_KV):
        a = cols(base + kv * LANES, base + (kv + 1) * LANES)
        vsa_ref[kv] = jnp.where(ones_col, 1.0, a).astype(BF16)
        a = cols(base + (NSA_KV + kv) * LANES, base + (NSA_KV + kv + 1) * LANES)
        vwa_ref[kv] = jnp.where(ones_col, 1.0, a).astype(BF16)
    ksbt_ref[...] = _dot_nt(wt_ref[0:512, :], xb).astype(BF16)
    kst_ref[...] = _dot_nt(wt_ref[512:640, :], xb).astype(BF16)
    kwt_ref[...] = _dot_nt(wt_ref[640:768, :], xb).astype(BF16)


def _proj(x, g, w_all, w_t, tm):
    s = x.shape[0]
    nw = w_all.shape[1]
    row = lambda n: pl.BlockSpec((tm, n), lambda i: (i, 0))
    head = pl.BlockSpec((SB_HEADS, tm, HEAD_DIM), lambda i: (0, i, 0))
    aug = pl.BlockSpec((NSA_KV, tm, LANES), lambda i: (0, i, 0))
    tr = lambda n: pl.BlockSpec((n, tm), lambda i: (0, i))
    sd = jax.ShapeDtypeStruct
    out_shape = (
        sd((SB_HEADS, s, HEAD_DIM), BF16), sd((s, SB_WIDTH), F32), sd((s, SB_WIDTH), F32),
        sd((s, SB_WIDTH), BF16), sd((NSA_HEADS, s, HEAD_DIM), BF16),
        *(sd((s, KV_WIDTH), F32) for _ in range(6)),
        sd((NSA_KV, s, LANES), BF16), sd((NSA_KV, s, LANES), BF16),
        sd((SB_WIDTH, s), BF16), sd((KV_WIDTH, s), BF16), sd((KV_WIDTH, s), BF16))
    out_specs = (head, row(SB_WIDTH), row(SB_WIDTH), row(SB_WIDTH), head,
                 *(row(KV_WIDTH) for _ in range(6)), aug, aug,
                 tr(SB_WIDTH), tr(KV_WIDTH), tr(KV_WIDTH))
    return pl.pallas_call(
        _proj_kernel,
        grid=(s // tm,),
        in_specs=[row(D_MODEL), pl.BlockSpec((1, D_MODEL), lambda i: (0, 0)),
                  pl.BlockSpec((D_MODEL, nw), lambda i: (0, 0)),
                  pl.BlockSpec((w_t.shape[0], D_MODEL), lambda i: (0, 0))],
        out_specs=out_specs,
        out_shape=out_shape,
        compiler_params=_cparams(("parallel",)),
        name="proj_prompt",
    )(x, g, w_all, w_t)


def _proj_s_kernel(x_ref, g_ref, w_ref, o_ref):
    o_ref[...] = _dot(_rms(x_ref[...], g_ref[...]).astype(BF16), w_ref[...])


def _proj_s(x, g, w):
    b, n = x.shape[0], w.shape[1]
    return pl.pallas_call(
        _proj_s_kernel,
        out_shape=jax.ShapeDtypeStruct((b, n), F32),
        compiler_params=pltpu.CompilerParams(vmem_limit_bytes=VMEM_LIMIT),
        name="proj_sample",
    )(x, g, w)


def _sb_kernel(q_ref, kt_ref, v_ref, tri_ref, o_ref, *, tq):
    qi = pl.program_id(1)
    tri = tri_ref[...]
    row = lax.broadcasted_iota(jnp.int32, (tq, tq), 0)
    col = lax.broadcasted_iota(jnp.int32, (tq, tq), 1)
    below = col < row
    res = []
    for hh in range(2):
        q = q_ref[hh]

        def tile(j, run, diagonal, hh=hh, q=q):
            ks = pl.multiple_of(j * tq, tq)
            kt = kt_ref[hh * HEAD_DIM:(hh + 1) * HEAD_DIM, pl.ds(ks, tq)]
            z = _dot(q, kt)
            t = jnp.log(1.0 + jnp.exp(-jnp.abs(z)))
            log_keep = -(jnp.maximum(z, 0.0) + t)
            if diagonal:
                log_keep = jnp.where(below, log_keep, 0.0)
            after = _dot(log_keep.astype(BF16), tri)
            a = jnp.exp(jnp.minimum(z, 0.0) - t + after + run)
            if diagonal:
                a = jnp.where(below, a, 0.0)
            pv = _dot(a.astype(BF16), v_ref[pl.ds(ks, tq), :])
            return pv, run + after[:, 0:1] + log_keep[:, 0:1]

        pv0, run0 = tile(qi, jnp.zeros((tq, 1), F32), True)

        def cond(c):
            return jnp.logical_and(c[0] <= qi, c[3])

        def body(c):
            jj, acc, run, _ = c
            pv, run = tile(qi - jj, run, False)
            return jj + 1, acc + pv, run, jnp.max(run) > EXP_UNDERFLOW

        c = lax.while_loop(cond, body, (jnp.int32(1), pv0, run0, jnp.max(run0) > EXP_UNDERFLOW))
        res.append(c[1])
    lane = lax.broadcasted_iota(jnp.int32, (tq, LANES), 1)
    o_ref[...] = jnp.where(lane < HEAD_DIM, res[0], res[1])


def _sb_prompt(q_h, k_t, v_b, tq):
    s = k_t.shape[1]
    tri = jnp.asarray(np.tril(np.ones((tq, tq), np.float32), -1), BF16)
    return pl.pallas_call(
        functools.partial(_sb_kernel, tq=tq),
        grid=(SB_HEADS // 2, s // tq),
        in_specs=[pl.BlockSpec((2, tq, HEAD_DIM), lambda p, i: (p, i, 0)),
                  pl.BlockSpec((LANES, s), lambda p, i: (p, 0)),
                  pl.BlockSpec((s, LANES), lambda p, i: (0, p)),
                  pl.BlockSpec((tq, tq), lambda p, i: (0, 0))],
        out_specs=pl.BlockSpec((tq, LANES), lambda p, i: (i, p)),
        out_shape=jax.ShapeDtypeStruct((s, SB_WIDTH), F32),
        compiler_params=_cparams(("parallel", "parallel")),
        name="sb_prompt",
    )(q_h, k_t, v_b, tri)


def _compress_rows(x, pe_a, pe_b, w_a, w_b, b1, w2):
    m = x.shape[0]
    a = _dot((x + pe_a).astype(BF16), w_a)
    b = _dot((x + pe_b).astype(BF16), w_b)
    b = pltpu.roll(b, m - 1, 0)
    return jax.nn.gelu(a + b + b1)


def _compress_kernel(x_ref, pea_ref, peb_ref, wa_ref, wb_ref, b1_ref, w2_ref, w2t_ref, w2a_ref,
                     cmp_t_ref, cmp_a_ref):
    hid = _compress_rows(x_ref[...], pea_ref[...], peb_ref[...], wa_ref[...], wb_ref[...],
                         b1_ref[...], None).astype(BF16)
    cmp_t_ref[...] = _dot_nt(w2t_ref[...], hid).astype(BF16)
    for kv in range(NSA_KV):
        cmp_a_ref[kv] = _dot(hid, w2a_ref[kv]).astype(BF16)


def _compress_prompt(x2, cw):
    m = x2.shape[1]
    lead = lambda *shape: pl.BlockSpec((None,) + shape, lambda i: (i,) + (0,) * len(shape))
    return pl.pallas_call(
        _compress_kernel,
        grid=(2,),
        in_specs=[lead(m, 16 * KV_WIDTH), lead(1, 16 * KV_WIDTH), lead(1, 16 * KV_WIDTH),
                  lead(16 * KV_WIDTH, 2 * CMP_HIDDEN), lead(16 * KV_WIDTH, 2 * CMP_HIDDEN),
                  lead(1, 2 * CMP_HIDDEN), lead(2 * CMP_HIDDEN, KV_WIDTH),
                  lead(KV_WIDTH, 2 * CMP_HIDDEN), lead(NSA_KV, 2 * CMP_HIDDEN, LANES)],
        out_specs=(lead(KV_WIDTH, m), lead(NSA_KV, m, LANES)),
        out_shape=(jax.ShapeDtypeStruct((2, KV_WIDTH, m), BF16),
                   jax.ShapeDtypeStruct((2, NSA_KV, m, LANES), BF16)),
        compiler_params=_cparams(("parallel",)),
        name="compress_prompt",
    )(x2, cw["pe_a"], cw["pe_b"], cw["w_a"], cw["w_b"], cw["b1"], cw["w2"], cw["w2t"], cw["w2a"])


def _compress_weights(pe, w1, b1, w2):
    half = CMP_BLOCK // 2
    eye = jnp.eye(NSA_KV, dtype=F32)
    w1r = w1.reshape(2, half, HEAD_DIM, CMP_HIDDEN)

    def big(wh):
        return jnp.einsum("rdf,pk->rpdkf", wh, eye).reshape(half * KV_WIDTH, NSA_KV * CMP_HIDDEN)

    def perow(p):
        return jnp.broadcast_to(p[:, None, :], (half, NSA_KV, HEAD_DIM)).reshape(1, half * KV_WIDTH)

    w2bd = jnp.einsum("fd,pk->pfkd", w2, eye).reshape(NSA_KV * CMP_HIDDEN, KV_WIDTH)
    w2a = jnp.stack([jnp.concatenate(
        [w2bd[:, kv * HEAD_DIM:(kv + 1) * HEAD_DIM], jnp.zeros((NSA_KV * CMP_HIDDEN, LANES - HEAD_DIM), F32)],
        axis=1) for kv in range(NSA_KV)])
    return dict(pe_a=perow(pe[:half]), pe_b=perow(pe[half:]),
                w_a=big(w1r[0]).astype(BF16), w_b=big(w1r[1]).astype(BF16),
                b1=jnp.tile(b1, NSA_KV)[None], w2=w2bd.astype(BF16), w2t=w2bd.T.astype(BF16),
                w2a=w2a.astype(BF16))


def _top_select(vals, idx, forced, axis, rounds):
    sel = forced
    for _ in range(rounds):
        mx = jnp.max(vals, axis=axis, keepdims=True)
        first = jnp.min(jnp.where(vals == mx, idx, jnp.int32(1 << 30)), axis=axis, keepdims=True)
        hit = jnp.logical_and(idx == first, mx > 0.5 * NEG)
        sel = jnp.logical_or(sel, hit)
        vals = jnp.where(hit, NEG, vals)
    return sel


def _pack_heads(o, tq):
    lane = lax.broadcasted_iota(jnp.int32, (tq, LANES), 1)
    o4 = o.reshape(NSA_GROUP, tq, LANES)
    halves = []
    for a in range(NSA_GROUP // 2):
        halves.append(jnp.where(lane < HEAD_DIM, o4[2 * a], pltpu.roll(o4[2 * a + 1], HEAD_DIM, 1)))
    return halves


def _nsa_cmp_kernel(q_ref, ltab_ref, kt_ref, v_ref, ov_ref, o_ref, sel_ref, *, tq, n_blk):
    qi = pl.program_id(1)
    rows = NSA_GROUP * tq
    n_c = kt_ref.shape[1]
    q4 = q_ref[...].reshape(rows, HEAD_DIM)
    r = lax.broadcasted_iota(jnp.int32, (HEAD_DIM, n_c), 0)
    c = lax.broadcasted_iota(jnp.int32, (HEAD_DIM, n_c), 1)
    place = jnp.where(jnp.logical_and(r < 32, c == (tq // CMP_STRIDE) * qi + (r & 15) - 9), 1.0, 0.0)
    s = _dot(q4, kt_ref[...]) + _dot(ltab_ref[...], place.astype(BF16))
    qpos = qi * tq + (lax.broadcasted_iota(jnp.int32, (rows, 1), 0) & (tq - 1))
    c_max = (qpos - (CMP_BLOCK - 1)) >> 4
    mask = lax.broadcasted_iota(jnp.int32, (rows, n_c), 1) <= c_max
    s = jnp.where(mask, s, NEG)
    m = jnp.max(s, axis=-1, keepdims=True)
    e = jnp.where(mask, jnp.exp(s - m), 0.0)
    p = e / jnp.maximum(jnp.sum(e, axis=-1, keepdims=True), 1e-30)
    o = _dot(p.astype(BF16), v_ref[...])
    h0, h1 = _pack_heads(o, tq)
    o_ref[:, 0:LANES] = h0
    o_ref[:, LANES:2 * LANES] = h1
    p4 = p.reshape(NSA_GROUP, tq, n_c)
    pg = p4[0] + p4[1] + p4[2] + p4[3]
    imp = _dot(pg.astype(BF16), ov_ref[...])
    j = lax.broadcasted_iota(jnp.int32, (tq, n_blk), 1)
    q_blk = (qi * tq + lax.broadcasted_iota(jnp.int32, (tq, 1), 0)) >> 6
    forced = jnp.logical_or(j == 0, jnp.logical_and(j >= q_blk - 1, j <= q_blk))
    cand = jnp.logical_and(j >= 1, j <= q_blk - 2)
    sel = _top_select(jnp.where(cand, imp, NEG), j, forced, 1, N_SELECT - N_FORCED)
    sel_ref[...] = jnp.where(sel, 1.0, 0.0).astype(BF16)


def _nsa_cmp_prompt(qn_h, ltab, cmp_kt, cmp_va, overlap, tq):
    s = qn_h.shape[1]
    n_c = cmp_kt.shape[2]
    n_blk = overlap.shape[1]
    return pl.pallas_call(
        functools.partial(_nsa_cmp_kernel, tq=tq, n_blk=n_blk),
        grid=(NSA_KV, s // tq),
        in_specs=[pl.BlockSpec((NSA_GROUP, tq, HEAD_DIM), lambda k, i: (k, i, 0)),
                  pl.BlockSpec((None, NSA_GROUP * tq, HEAD_DIM), lambda k, i: (k, 0, 0)),
                  pl.BlockSpec((None, HEAD_DIM, n_c), lambda k, i: (k, 0, 0)),
                  pl.BlockSpec((None, n_c, LANES), lambda k, i: (k, 0, 0)),
                  pl.BlockSpec((n_c, n_blk), lambda k, i: (0, 0))],
        out_specs=(pl.BlockSpec((tq, 2 * LANES), lambda k, i: (i, k)),
                   pl.BlockSpec((None, tq, n_blk), lambda k, i: (k, i, 0))),
        out_shape=(jax.ShapeDtypeStruct((s, NSA_WIDTH), F32),
                   jax.ShapeDtypeStruct((NSA_KV, s, n_blk), BF16)),
        compiler_params=_cparams(("parallel", "parallel")),
        name="nsa_cmp_prompt",
    )(qn_h, ltab, cmp_kt, cmp_va, overlap)


def _nsa_sel_kernel(q_ref, kt_ref, v_ref, sel_ref, pn_ref, o_ref, *, tq, n_blk):
    qi = pl.program_id(1)
    rows = NSA_GROUP * tq
    tk = 2 * tq
    q4 = q_ref[...].reshape(rows, HEAD_DIM)
    selm = sel_ref[...]
    jb = lax.broadcasted_iota(jnp.int32, (n_blk, tk), 0)
    lk = lax.broadcasted_iota(jnp.int32, (n_blk, tk), 1)

    def tile(start, bias, m, acc):
        ps = pl.multiple_of(start + SEL_PAD, LANES)
        expand = jnp.where(jb == ((start + lk) >> 6), 1.0, 0.0).astype(BF16)
        keep = (_dot(selm, expand) > 0.5)[None]
        s = _dot(q4, kt_ref[:, pl.ds(ps, tk)])
        if bias is not None:
            s = s + bias
        s3 = jnp.where(keep, s.reshape(NSA_GROUP, tq, tk), NEG)
        m_new = jnp.maximum(m, jnp.max(s3, axis=-1, keepdims=True))
        p = jnp.where(keep, jnp.exp(s3 - m_new), 0.0).reshape(rows, tk)
        alpha = jnp.exp(m - m_new).reshape(rows, 1)
        acc = alpha * acc + _dot(p.astype(BF16), v_ref[pl.ds(ps, tk), :])
        return m_new, acc

    near = (qi - 1) * tq
    m0 = jnp.full((NSA_GROUP, tq, 1), NEG, F32)
    m, acc = tile(near, pn_ref[...], m0, jnp.zeros((rows, LANES), F32))

    def body(t, c):
        return tile(near - tk * (t + 1), None, *c)

    m, acc = lax.fori_loop(0, qi // 2, body, (m, acc))
    o = acc / jnp.maximum(acc[:, HEAD_DIM:HEAD_DIM + 1], 1e-30)
    h0, h1 = _pack_heads(o, tq)
    o_ref[:, 0:LANES] = h0
    o_ref[:, LANES:2 * LANES] = h1


def _nsa_sel_prompt(qn_h, ks_t, vs_a, sel, p_near, tq):
    s = qn_h.shape[1]
    sp = ks_t.shape[2]
    n_blk = sel.shape[2]
    return pl.pallas_call(
        functools.partial(_nsa_sel_kernel, tq=tq, n_blk=n_blk),
        grid=(NSA_KV, s // tq),
        in_specs=[pl.BlockSpec((NSA_GROUP, tq, HEAD_DIM), lambda k, i: (k, i, 0)),
                  pl.BlockSpec((None, HEAD_DIM, sp), lambda k, i: (k, 0, 0)),
                  pl.BlockSpec((None, sp, LANES), lambda k, i: (k, 0, 0)),
                  pl.BlockSpec((None, tq, n_blk), lambda k, i: (k, i, 0)),
                  pl.BlockSpec((None, NSA_GROUP * tq, 2 * tq), lambda k, i: (k, 0, 0))],
        out_specs=pl.BlockSpec((tq, 2 * LANES), lambda k, i: (i, k)),
        out_shape=jax.ShapeDtypeStruct((s, NSA_WIDTH), F32),
        compiler_params=_cparams(("parallel", "parallel")),
        name="nsa_sel_prompt",
    )(qn_h, ks_t, vs_a, sel, p_near)


def _nsa_win_kernel(q_ref, kt_ref, v_ref, pw_ref, o_ref, *, tq):
    qi = pl.program_id(1)
    rows = NSA_GROUP * tq
    span = WINDOW + tq
    q4 = q_ref[...].reshape(rows, HEAD_DIM)
    ps = pl.multiple_of(qi * tq, LANES)
    s = _dot(q4, kt_ref[:, pl.ds(ps, span)]) + pw_ref[...]
    real = lax.broadcasted_iota(jnp.int32, (rows, span), 1) >= WIN_PAD - qi * tq
    s = jnp.where(real, s, NEG)
    m = jnp.max(s, axis=-1, keepdims=True)
    e = jnp.where(real, jnp.exp(s - m), 0.0)
    acc = _dot(e.astype(BF16), v_ref[pl.ds(ps, span), :])
    o = acc / jnp.maximum(acc[:, HEAD_DIM:HEAD_DIM + 1], 1e-30)
    h0, h1 = _pack_heads(o, tq)
    o_ref[:, 0:LANES] = h0
    o_ref[:, LANES:2 * LANES] = h1


def _nsa_win_prompt(qn_h, kw_t, vw_a, p_win, tq):
    s = qn_h.shape[1]
    sp = kw_t.shape[2]
    return pl.pallas_call(
        functools.partial(_nsa_win_kernel, tq=tq),
        grid=(NSA_KV, s // tq),
        in_specs=[pl.BlockSpec((NSA_GROUP, tq, HEAD_DIM), lambda k, i: (k, i, 0)),
                  pl.BlockSpec((None, HEAD_DIM, sp), lambda k, i: (k, 0, 0)),
                  pl.BlockSpec((None, sp, LANES), lambda k, i: (k, 0, 0)),
                  pl.BlockSpec((None, NSA_GROUP * tq, WINDOW + tq), lambda k, i: (k, 0, 0))],
        out_specs=pl.BlockSpec((tq, 2 * LANES), lambda k, i: (i, k)),
        out_shape=jax.ShapeDtypeStruct((s, NSA_WIDTH), F32),
        compiler_params=_cparams(("parallel", "parallel")),
        name="nsa_win_prompt",
    )(qn_h, kw_t, vw_a, p_win)


def _route(logits):
    lane = lax.broadcasted_iota(jnp.int32, logits.shape, 1)
    big = jnp.int32(1 << 30)
    is_g = jnp.logical_and(lane >= N_EXPERTS, lane < N_EXPERTS + N_GROUPS)
    gl = jnp.where(is_g, logits, NEG)
    gmax = jnp.max(gl, axis=-1, keepdims=True)
    gidx = jnp.min(jnp.where(gl == gmax, lane - N_EXPERTS, big), axis=-1, keepdims=True)
    g_w = 1.0 / jnp.sum(jnp.where(is_g, jnp.exp(gl - gmax), 0.0), axis=-1, keepdims=True)
    in_g = jnp.logical_and(lane < N_EXPERTS, (lane >> 3) == gidx)
    el = jnp.where(in_g, logits, NEG)
    emax = jnp.max(el, axis=-1, keepdims=True)
    ee = jnp.where(in_g, jnp.exp(el - emax), 0.0)
    prob = jnp.where(in_g, ee / jnp.sum(ee, axis=-1, keepdims=True), -1.0)
    p1 = jnp.max(prob, axis=-1, keepdims=True)
    i1 = jnp.min(jnp.where(prob == p1, lane, big), axis=-1, keepdims=True)
    prob2 = jnp.where(lane == i1, -1.0, prob)
    p2 = jnp.max(prob2, axis=-1, keepdims=True)
    i2 = jnp.min(jnp.where(prob2 == p2, lane, big), axis=-1, keepdims=True)
    tot = p1 + p2
    return jnp.where(lane == i1, g_w * (p1 / tot), jnp.where(lane == i2, g_w * (p2 / tot), 0.0))


def _merge_kernel(x_ref, osb_ref, oc_ref, os_ref, ow_ref, gmix_ref, wg_ref, eg_ref, wbs_ref, wbn_ref,
                  wo_ref, gffn_ref, wr_ref, h_ref, hn_ref, comb_ref):
    mm = lambda a, w_ref: _dot(a.astype(BF16), w_ref[...])
    x = x_ref[...]
    u = mm(_rms(x, gmix_ref[...]), wg_ref)
    g = jax.nn.sigmoid(u[:, 0:LANES])
    g_hi = _r16(g)
    g_mid = _r16(g - g_hi)
    g_br = mm(g_hi, eg_ref) + mm(g_mid, eg_ref) + mm(g - g_hi - g_mid, eg_ref)
    o_nsa = (g_br[:, 0:NSA_WIDTH] * oc_ref[...] + g_br[:, NSA_WIDTH:2 * NSA_WIDTH] * os_ref[...]
             + g_br[:, 2 * NSA_WIDTH:3 * NSA_WIDTH] * ow_ref[...])
    merged = (jax.nn.sigmoid(u[:, LANES:LANES + D_MODEL]) * mm(osb_ref[...], wbs_ref)
              + jax.nn.sigmoid(u[:, LANES + D_MODEL:LANES + 2 * D_MODEL]) * mm(o_nsa, wbn_ref))
    h = x + mm(merged, wo_ref)
    hn = _rms(h, gffn_ref[...])
    h_ref[...] = h
    hn_ref[...] = hn.astype(BF16)
    comb_ref[...] = _route(mm(hn, wr_ref))


def _merge(x, o_sb, o_c, o_s, o_w, mw, tm):
    n = x.shape[0]
    row = lambda c: pl.BlockSpec((tm, c), lambda i: (i, 0))
    full = lambda a: pl.BlockSpec(a.shape, lambda i: (0,) * a.ndim)
    ws = (mw["g_mix"], mw["w_gate"], mw["e_g"], mw["w_b_sb"], mw["w_b_nsa"], mw["w_o"], mw["g_ffn"], mw["w_r"])
    return pl.pallas_call(
        _merge_kernel,
        grid=(n // tm,),
        in_specs=[row(D_MODEL), row(SB_WIDTH), row(NSA_WIDTH), row(NSA_WIDTH), row(NSA_WIDTH)]
                 + [full(a) for a in ws],
        out_specs=(row(D_MODEL), row(D_MODEL), row(LANES)),
        out_shape=(jax.ShapeDtypeStruct((n, D_MODEL), F32), jax.ShapeDtypeStruct((n, D_MODEL), BF16),
                   jax.ShapeDtypeStruct((n, LANES), F32)),
        compiler_params=_cparams(("parallel",)),
        name="merge",
    )(x, o_sb, o_c, o_s, o_w, *ws)


def _moe_kernel(h_ref, hn_ref, comb_ref, pe_ref, weg_ref, weu_ref, wed_ref, gple_ref, wpg_ref, wpe_ref,
                gfin_ref, y_ref, acc_ref):
    e = pl.program_id(1)

    @pl.when(e == 0)
    def _():
        acc_ref[...] = jnp.zeros_like(acc_ref)

    hn = hn_ref[...]
    lane = lax.broadcasted_iota(jnp.int32, comb_ref.shape, 1)
    ce = jnp.sum(jnp.where(lane == e, comb_ref[...], 0.0), axis=-1, keepdims=True)
    hid = jax.nn.silu(_dot(hn, weg_ref[...])) * _dot(hn, weu_ref[...]) * ce
    acc_ref[...] += _dot(hid.astype(BF16), wed_ref[...])

    @pl.when(e == N_EXPERTS - 1)
    def _():
        h = h_ref[...] + acc_ref[...]
        gate = jax.nn.sigmoid(_dot(_rms(h, gple_ref[...]).astype(BF16), wpg_ref[...]))
        h = h + gate * _dot(pe_ref[...].astype(BF16), wpe_ref[...])
        y_ref[...] = _rms(h, gfin_ref[...])


def _moe(h, hn, comb, p_emb, ew, tm):
    n = h.shape[0]
    row = lambda c: pl.BlockSpec((tm, c), lambda i, e: (i, 0))
    full = lambda a: pl.BlockSpec(a.shape, lambda i, e: (0,) * a.ndim)
    exp = lambda a, b: pl.BlockSpec((None, a, b), lambda i, e: (e, 0, 0))
    return pl.pallas_call(
        _moe_kernel,
        grid=(n // tm, N_EXPERTS),
        in_specs=[row(D_MODEL), row(D_MODEL), row(LANES), row(PLE_DIM),
                  exp(D_MODEL, EXPERT_FF), exp(D_MODEL, EXPERT_FF), exp(EXPERT_FF, D_MODEL),
                  full(ew["g_ple"]), full(ew["w_pg"]), full(ew["w_pe"]), full(ew["g_fin"])],
        out_specs=row(D_MODEL),
        out_shape=jax.ShapeDtypeStruct((n, D_MODEL), F32),
        scratch_shapes=[pltpu.VMEM((tm, D_MODEL), F32)],
        compiler_params=_cparams(("parallel", "arbitrary")),
        name="moe_ple",
    )(h, hn, comb, p_emb, ew["w_eg"], ew["w_eu"], ew["w_ed"], ew["g_ple"], ew["w_pg"], ew["w_pe"], ew["g_fin"])


def _sample1_kernel(pt_ref, qsb_ref, qp_ref, sbk_ref, sbv_ref, cck_ref, ccv_ref,
                    ssb_ref, tri_ref, snsa_ref, pea_ref, peb_ref, wa_ref, wb_ref, b1_ref, w2_ref,
                    w2a_ref, bc_ref, ovt_ref, gs_ref,
                    osb_ref, oc_ref, sel_ref,
                    x2k_ref, x2v_ref, acc_ref, run_ref, *, n_pages, n_c, n_sel):
    del pt_ref
    p = pl.program_id(1)
    page = n_pages - 1 - p

    @pl.when(p == 0)
    def _():
        acc_ref[...] = jnp.zeros_like(acc_ref)
        run_ref[...] = jnp.zeros_like(run_ref)

    z = _dot_hp(_r16(sbk_ref[...]) * _r16(qsb_ref[...]), ssb_ref[...])
    t = jnp.log(1.0 + jnp.exp(-jnp.abs(z)))
    log_keep = -(jnp.maximum(z, 0.0) + t)
    after = _dot_hp(tri_ref[...], log_keep)
    a = jnp.exp(jnp.minimum(z, 0.0) - t + after + run_ref[...])
    acc_ref[...] += _dot_tn_hp(_r16(a), _r16(sbv_ref[...]))
    run_ref[...] += jnp.sum(log_keep, axis=0, keepdims=True)

    rows = pl.ds(pl.multiple_of(page * 8, 8), 8)
    x2k_ref[rows, :] = cck_ref[...]
    x2v_ref[rows, :] = ccv_ref[...]

    @pl.when(p == n_pages - 1)
    def _():
        osb_ref[...] = acc_ref[0:8, :]
        cmp = []
        for i, x2 in enumerate((x2k_ref, x2v_ref)):
            hid = _compress_rows(x2[...], pea_ref[i], peb_ref[i], wa_ref[i], wb_ref[i], b1_ref[i], None)
            cmp.append(hid.astype(BF16))
        cmp_k = _dot(cmp[0], w2_ref[0])
        cmp_v = _dot(cmp[1], w2_ref[1])
        m_rows = cmp_k.shape[0]
        k4 = jnp.concatenate([_r16(cmp_k)] * NSA_GROUP, axis=1)
        s = _dot_hp(k4 * _r16(qp_ref[...]), snsa_ref[...]) + bc_ref[...]
        valid = lax.broadcasted_iota(jnp.int32, s.shape, 0) < n_c
        s = jnp.where(valid, s, NEG)
        m = jnp.max(s, axis=0, keepdims=True)
        e = jnp.where(valid, jnp.exp(s - m), 0.0)
        pc = e / jnp.maximum(jnp.sum(e, axis=0, keepdims=True), 1e-30)
        oc_ref[...] = _dot_tn_hp(_r16(pc), _r16(cmp_v))[0:8, :]
        imp = _dot_hp(ovt_ref[...], _r16(_dot_hp(pc, gs_ref[...])))
        j = lax.broadcasted_iota(jnp.int32, imp.shape, 0)
        forced = jnp.logical_or(j == 0, jnp.logical_and(j >= n_sel - 2, j < n_sel))
        cand = jnp.logical_and(j >= 1, j < n_sel - 2)
        sel = _top_select(jnp.where(cand, imp, NEG), j, forced, 0, N_SELECT - N_FORCED)
        sel_ref[...] = jnp.where(sel, 1.0, 0.0).reshape(sel_ref.shape)


def _sample1(pt, q_sb, q_perm, sbk, sbv, cck, ccv, consts, n_c, n_sel):
    b = q_sb.shape[0]
    n_pages = pt.shape[0] // b
    m_rows = n_pages * 8
    n_blk_pad = consts["ov_t"].shape[0]
    page_map = lambda bb, p, pt_ref: (pt_ref[bb * n_pages + n_pages - 1 - p], 0, 0)
    seq = lambda *shape: pl.BlockSpec((None,) + shape, lambda bb, p, pt_ref: (bb,) + (0,) * len(shape))
    full = lambda a: pl.BlockSpec(a.shape, lambda bb, p, pt_ref: (0,) * a.ndim)
    cs = [consts[k] for k in ("s_sb", "tri", "s_nsa", "pe_a", "pe_b", "w_a", "w_b", "b1", "w2", "w2a",
                              "b_cmp", "ov_t", "g_sum")]
    grid_spec = pltpu.PrefetchScalarGridSpec(
        num_scalar_prefetch=1,
        grid=(b, n_pages),
        in_specs=[seq(1, SB_WIDTH), seq(1, NSA_WIDTH),
                  pl.BlockSpec((None, PAGE, SB_WIDTH), page_map),
                  pl.BlockSpec((None, PAGE, SB_WIDTH), page_map),
                  pl.BlockSpec((None, 8, 16 * KV_WIDTH), page_map),
                  pl.BlockSpec((None, 8, 16 * KV_WIDTH), page_map)] + [full(a) for a in cs],
        out_specs=(seq(8, SB_WIDTH), seq(8, LANES), seq(n_blk_pad // 8, 8, LANES)),
        scratch_shapes=[pltpu.VMEM((m_rows, 16 * KV_WIDTH), F32), pltpu.VMEM((m_rows, 16 * KV_WIDTH), F32),
                        pltpu.VMEM((LANES, SB_WIDTH), F32), pltpu.VMEM((1, LANES), F32)])
    return pl.pallas_call(
        functools.partial(_sample1_kernel, n_pages=n_pages, n_c=n_c, n_sel=n_sel),
        grid_spec=grid_spec,
        out_shape=(jax.ShapeDtypeStruct((b, 8, SB_WIDTH), F32), jax.ShapeDtypeStruct((b, 8, LANES), F32),
                   jax.ShapeDtypeStruct((b, n_blk_pad // 8, 8, LANES), F32)),
        compiler_params=_cparams(("parallel", "arbitrary")),
        name="sample_pass1",
    )(pt, q_sb, q_perm, sbk, sbv, cck, ccv, *cs)


def _sample2_kernel(pt_ref, qp_ref, sk_ref, sv_ref, sel_ref, blast_ref, snsa_ref,
                    nks_ref, nvs_ref, nkw_ref, nvw_ref, wk_ref, wv_ref, bw_ref, bnew_ref,
                    os_ref, ow_ref, s_scr, v_scr, *, n_pages):
    del pt_ref
    p = pl.program_id(1)
    q_row = _r16(qp_ref[...])

    def scores(k):
        k4 = jnp.concatenate([_r16(k)] * NSA_GROUP, axis=1)
        return _dot_hp(k4 * q_row, snsa_ref[...])

    tile = sel_ref[p >> 2]
    sub = lax.broadcasted_iota(jnp.int32, tile.shape, 0)
    r0 = jnp.sum(jnp.where(sub == 2 * (p & 3), tile, 0.0), axis=0, keepdims=True)
    r1 = jnp.sum(jnp.where(sub == 2 * (p & 3) + 1, tile, 0.0), axis=0, keepdims=True)
    tok = lax.broadcasted_iota(jnp.int32, (PAGE, LANES), 0)
    keep = jnp.where(tok < SEL_BLOCK, r0, r1) > 0.5
    bias = jnp.where(p == n_pages - 1, blast_ref[...], 0.0)
    s_scr[p] = jnp.where(keep, scores(sk_ref[...]) + bias, NEG)
    v_scr[pl.ds(pl.multiple_of(p * PAGE, PAGE), PAGE), :] = sv_ref[...].astype(BF16)

    @pl.when(p == n_pages - 1)
    def _():
        first = tok == 0

        def attend(s_past, v_past, k_new_ref, v_new_ref):
            s_new = jnp.where(first, scores(jnp.where(first, k_new_ref[...], 0.0)) + bnew_ref[...], NEG)
            m = jnp.maximum(jnp.max(s_past, axis=0, keepdims=True), jnp.max(s_new, axis=0, keepdims=True))
            e_past = jnp.where(s_past > 0.5 * NEG, jnp.exp(s_past - m), 0.0)
            e_new = jnp.where(first, jnp.exp(s_new - m), 0.0)
            l = jnp.maximum(jnp.sum(e_past, axis=0, keepdims=True) + jnp.sum(e_new, axis=0, keepdims=True), 1e-30)
            o = _dot((e_past / l).T.astype(BF16), v_past)
            o = o + _r16(e_new / l).T[:, 0:1] * _r16(v_new_ref[...])
            return o[0:8, :]

        os_ref[...] = attend(s_scr[...].reshape(n_pages * PAGE, LANES), v_scr[...], nks_ref, nvs_ref)
        in_win = lax.broadcasted_iota(jnp.int32, (wk_ref.shape[0], LANES), 0) >= 1
        s_w = jnp.where(in_win, scores(wk_ref[...]) + bw_ref[...], NEG)
        ow_ref[...] = attend(s_w, wv_ref[...].astype(BF16), nkw_ref, nvw_ref)


def _sample2(pt, q_perm, sk, sv, sel, new_rows, win_k, win_v, consts):
    b = q_perm.shape[0]
    n_pages = pt.shape[0] // b
    w_rows = win_k.shape[1]
    page_map = lambda bb, p, pt_ref: (pt_ref[bb * n_pages + p], 0, 0)
    seq = lambda *shape: pl.BlockSpec((None,) + shape, lambda bb, p, pt_ref: (bb,) + (0,) * len(shape))
    full = lambda a: pl.BlockSpec(a.shape, lambda bb, p, pt_ref: (0,) * a.ndim)
    grid_spec = pltpu.PrefetchScalarGridSpec(
        num_scalar_prefetch=1,
        grid=(b, n_pages),
        in_specs=[seq(1, NSA_WIDTH),
                  pl.BlockSpec((None, PAGE, KV_WIDTH), page_map),
                  pl.BlockSpec((None, PAGE, KV_WIDTH), page_map),
                  seq(*sel.shape[1:]), full(consts["b_last"]), full(consts["s_nsa"]),
                  seq(1, KV_WIDTH), seq(1, KV_WIDTH), seq(1, KV_WIDTH), seq(1, KV_WIDTH),
                  seq(w_rows, KV_WIDTH), seq(w_rows, KV_WIDTH), full(consts["b_win"]), full(consts["b_new"])],
        out_specs=(seq(8, LANES), seq(8, LANES)),
        scratch_shapes=[pltpu.VMEM((n_pages, PAGE, LANES), F32), pltpu.VMEM((n_pages * PAGE, KV_WIDTH), BF16)])
    return pl.pallas_call(
        functools.partial(_sample2_kernel, n_pages=n_pages),
        grid_spec=grid_spec,
        out_shape=(jax.ShapeDtypeStruct((b, 8, LANES), F32), jax.ShapeDtypeStruct((b, 8, LANES), F32)),
        compiler_params=_cparams(("parallel", "arbitrary")),
        name="sample_pass2",
    )(pt, q_perm, sk, sv, sel, consts["b_last"], consts["s_nsa"], *new_rows, win_k, win_v,
      consts["b_win"], consts["b_new"])


def _bias_delta(table):
    bucket = _t5_bucket_np(np.arange(MAX_DISTANCE))
    return (table[bucket] - table[N_BUCKETS - 1][None]).T


def _dist_bias(delta, heads, dist, lo=0, hi=None):
    near = (dist >= 0) & (dist < MAX_DISTANCE)
    val = jnp.where(jnp.asarray(near), delta[heads, np.clip(dist, 0, MAX_DISTANCE - 1)], 0.0)
    bad = dist < lo if hi is None else (dist < lo) | (dist >= hi)
    return jnp.where(jnp.asarray(bad), NEG, val)


def _pad_lanes(a, n=LANES):
    return jnp.concatenate([a, jnp.zeros(a.shape[:-1] + (n - a.shape[-1],), a.dtype)], axis=-1)


def kernel(x_prompt, x_sample, cache_sb_k, cache_sb_v, cache_cmp_k, cache_cmp_v, cache_sel_k, cache_sel_v, state_win_k, state_win_v, page_table, p_prompt, p_sample, norm_mix, w_in, cmp_pe_k, cmp_w1_k, cmp_b1_k, cmp_w2_k, cmp_pe_v, cmp_w1_v, cmp_b1_v, cmp_w2_v, rel_bias_table, w_branch_sb, w_branch_nsa, w_out, norm_ffn, w_router_group, w_router_expert, w_exp_gate, w_exp_up, w_exp_down, norm_ple, w_ple_gate, w_ple, norm_final):
    assert w_in.shape[0] == 1 and x_prompt.shape[0] == 1 and x_sample.shape[1] == 1
    seq = x_prompt.shape[1]
    n_seq = x_sample.shape[0]
    n_pool = cache_sb_k.shape[1]
    n_pages = page_table.shape[1]
    past = n_pages * PAGE
    w_buf = state_win_k.shape[2]
    tq = 128
    assert seq % 512 == 0 and w_buf == WINDOW and past >= WINDOW and seq >= WINDOW

    scale = HEAD_DIM ** -0.5
    w = w_in[0]
    o_g = 2048 + 6 * KV_WIDTH
    w_q_sb, w_k_sb, w_v_sb, w_q_n = (w[:, i * 512:(i + 1) * 512] for i in range(4))
    w_kv6 = w[:, 2048:o_g]
    w_vs, w_vw = w[:, 2048 + 3 * KV_WIDTH:2048 + 4 * KV_WIDTH], w[:, 2048 + 5 * KV_WIDTH:o_g]
    w_ks, w_kw = w[:, 2048 + 2 * KV_WIDTH:2048 + 3 * KV_WIDTH], w[:, 2048 + 4 * KV_WIDTH:2048 + 5 * KV_WIDTH]
    aug = lambda wv, kv: _pad_lanes(wv[:, kv * HEAD_DIM:(kv + 1) * HEAD_DIM])
    w_all = jnp.concatenate([w_q_sb * scale, w_k_sb, w_v_sb, w_q_n * scale, w_kv6,
                             aug(w_vs, 0), aug(w_vs, 1), aug(w_vw, 0), aug(w_vw, 1)], axis=1).astype(BF16)
    w_t = jnp.concatenate([w_k_sb.T, w_ks.T, w_kw.T], axis=0).astype(BF16)
    g_mix = norm_mix[0][None]
    w_gate_f = jnp.concatenate([_pad_lanes(w[:, o_g:o_g + 3 * NSA_HEADS]), w[:, o_g + 3 * NSA_HEADS:]], axis=1)
    e_g = np.zeros((LANES, 3 * NSA_WIDTH), np.float32)
    for h in range(NSA_HEADS):
        for br in range(3):
            e_g[h * 3 + br, br * NSA_WIDTH + h * HEAD_DIM:br * NSA_WIDTH + (h + 1) * HEAD_DIM] = 1.0
    w_r = jnp.concatenate([w_router_expert[0], w_router_group[0],
                           jnp.zeros((D_MODEL, LANES - N_EXPERTS - N_GROUPS), F32)], axis=1)
    mw = dict(g_mix=g_mix, w_gate=w_gate_f.astype(BF16), e_g=jnp.asarray(e_g, BF16),
              w_b_sb=w_branch_sb[0].astype(BF16), w_b_nsa=w_branch_nsa[0].astype(BF16),
              w_o=w_out[0].astype(BF16), g_ffn=norm_ffn[0][None], w_r=w_r.astype(BF16))
    ew = dict(w_eg=w_exp_gate[0].astype(BF16), w_eu=w_exp_up[0].astype(BF16), w_ed=w_exp_down[0].astype(BF16),
              g_ple=norm_ple[0][None], w_pg=w_ple_gate[0].astype(BF16), w_pe=w_ple[0].astype(BF16),
              g_fin=norm_final[None])
    cw_k = _compress_weights(cmp_pe_k[0], cmp_w1_k[0], cmp_b1_k[0], cmp_w2_k[0])
    cw_v = _compress_weights(cmp_pe_v[0], cmp_w1_v[0], cmp_b1_v[0], cmp_w2_v[0])
    cw = {k: jnp.stack([cw_k[k], cw_v[k]]) for k in cw_k}

    delta = _bias_delta(rel_bias_table)
    rows = NSA_GROUP * tq
    g_of = np.arange(rows) // tq
    i_of = np.arange(rows) % tq
    head_of = lambda kv: (kv * NSA_GROUP + g_of)[:, None]
    d_c = i_of[:, None] - CMP_STRIDE * (np.arange(16)[None] - 9) - (CMP_BLOCK - 1)
    l_tab = jnp.stack([_dist_bias(delta, head_of(kv), d_c) * jnp.asarray(d_c >= 0, F32) for kv in range(NSA_KV)])
    l_hi = l_tab.astype(BF16)
    l_lo = (l_tab - l_hi.astype(F32)).astype(BF16)
    l_tab = jnp.concatenate([l_hi, l_lo, jnp.zeros((NSA_KV, rows, HEAD_DIM - 32), BF16)], axis=-1)
    d_n = i_of[:, None] + tq - np.arange(2 * tq)[None]
    p_near = jnp.stack([_dist_bias(delta, head_of(kv), d_n) for kv in range(NSA_KV)])
    d_w = i_of[:, None] + WINDOW - np.arange(WINDOW + tq)[None]
    p_win = jnp.stack([_dist_bias(delta, head_of(kv), d_w, 0, WINDOW) for kv in range(NSA_KV)])

    xp = x_prompt[0]
    (q_sb_h, k_sb, v_sb, v_sb_b, q_n_h, k_c, v_c, k_s, v_s, k_w, v_w, vs_a, vw_a,
     k_sb_t, k_s_t, k_w_t) = _proj(xp, g_mix, w_all, w_t, 512)
    o_sb = _sb_prompt(q_sb_h, k_sb_t, v_sb_b, 256)

    x2 = jnp.stack([k_c, v_c]).reshape(2, seq // CMP_STRIDE, CMP_STRIDE * KV_WIDTH)
    cmp_t, cmp_a = _compress_prompt(x2, cw)
    n_cp = seq // CMP_STRIDE
    n_blk = seq // SEL_BLOCK
    c_start = np.arange(n_cp)[:, None] * CMP_STRIDE
    b_start = np.arange(n_blk)[None] * SEL_BLOCK
    overlap = ((c_start < b_start + SEL_BLOCK) & (c_start + CMP_BLOCK > b_start)
               & (np.arange(n_cp)[:, None] < n_cp - 1))
    overlap = jnp.asarray(overlap.astype(np.float32), BF16)
    o_c, sel = _nsa_cmp_prompt(q_n_h, l_tab, cmp_t[0].reshape(NSA_KV, HEAD_DIM, n_cp), cmp_a[1], overlap, tq)

    front = lambda a, n, axis: jnp.pad(a, [(n, 0) if ax == axis else (0, 0) for ax in range(a.ndim)])
    o_s = _nsa_sel_prompt(q_n_h, front(k_s_t.reshape(NSA_KV, HEAD_DIM, seq), SEL_PAD, 2),
                          front(vs_a, SEL_PAD, 1), sel, p_near, tq)
    o_w = _nsa_win_prompt(q_n_h, front(k_w_t.reshape(NSA_KV, HEAD_DIM, seq), WIN_PAD, 2),
                          front(vw_a, WIN_PAD, 1), p_win, tq)
    h_p, hn_p, comb_p = _merge(xp, o_sb, o_c, o_s, o_w, mw, 512)
    y_prompt = _moe(h_p, hn_p, comb_p, p_prompt[0, 0], ew, min(1024, seq))[None]

    perm = (np.arange(NSA_GROUP)[:, None, None] * HEAD_DIM + np.arange(NSA_KV)[None, :, None] * NSA_GROUP * HEAD_DIM
            + np.arange(HEAD_DIM)[None, None, :]).reshape(-1)
    w_s = jnp.concatenate([w_q_sb * scale, w_k_sb, w_v_sb, (w_q_n * scale)[:, perm], w_kv6],
                          axis=1).astype(BF16)
    xs = x_sample[:, 0]
    u = _proj_s(xs, g_mix, w_s)
    q_sb_s, k_sb_s, v_sb_s, q_perm = (u[:, i * 512:(i + 1) * 512] for i in range(4))
    kc_s, vc_s, ks_s, vs_s, kw_s, vw_s = (u[:, 2048 + i * KV_WIDTH:2048 + (i + 1) * KV_WIDTH] for i in range(6))

    n_c = (past + 1 - CMP_BLOCK) // CMP_STRIDE + 1
    n_sel = -(-(past + 1) // SEL_BLOCK)
    m_rows = past // CMP_STRIDE
    n_blk_pad = -(-n_sel // 32) * 32
    head8 = np.arange(LANES)[None, :] < NSA_HEADS
    hl = np.minimum(np.arange(LANES), NSA_HEADS - 1)[None, :]
    s_sb = np.zeros((SB_WIDTH, LANES), np.float32)
    s_sb[np.arange(SB_WIDTH), np.arange(SB_WIDTH) // HEAD_DIM] = 1.0
    s_nsa = np.zeros((NSA_WIDTH, LANES), np.float32)
    for g in range(NSA_GROUP):
        for kv in range(NSA_KV):
            s_nsa[g * KV_WIDTH + kv * HEAD_DIM:g * KV_WIDTH + (kv + 1) * HEAD_DIM, kv * NSA_GROUP + g] = 1.0
    tri_s = np.triu(np.ones((PAGE, PAGE), np.float32), 1)
    ov_t = np.zeros((n_blk_pad, m_rows), np.float32)
    cs = np.arange(n_c) * CMP_STRIDE
    for jb in range(n_sel):
        ov_t[jb, :n_c] = (cs < (jb + 1) * SEL_BLOCK) & (cs + CMP_BLOCK > jb * SEL_BLOCK)
    g_sum = np.zeros((LANES, LANES), np.float32)
    g_sum[:NSA_HEADS, :NSA_HEADS] = (np.arange(NSA_HEADS)[:, None] // NSA_GROUP
                                     == np.arange(NSA_HEADS)[None] // NSA_GROUP)
    mask8 = jnp.asarray(head8, F32)
    d_cs = (past - (np.arange(m_rows) * CMP_STRIDE + CMP_BLOCK - 1))[:, None] + 0 * hl
    b_cmp = _dist_bias(delta, hl + 0 * d_cs, np.maximum(d_cs, 0)) * mask8
    d_last = (PAGE - np.arange(PAGE))[:, None] + 0 * hl
    b_last = _dist_bias(delta, hl + 0 * d_last, d_last) * mask8
    d_ws = (w_buf - np.arange(w_buf))[:, None] + 0 * hl
    b_win = _dist_bias(delta, hl + 0 * d_ws, d_ws) * mask8
    b_new = _pad_lanes(delta[:, 0][None])
    consts = dict(s_sb=jnp.asarray(s_sb), tri=jnp.asarray(tri_s), s_nsa=jnp.asarray(s_nsa),
                  pe_a=cw["pe_a"], pe_b=cw["pe_b"], w_a=cw["w_a"], w_b=cw["w_b"], b1=cw["b1"], w2=cw["w2"],
                  w2a=cw["w2a"], b_cmp=b_cmp, ov_t=jnp.asarray(ov_t), g_sum=jnp.asarray(g_sum),
                  b_last=b_last, b_win=b_win, b_new=b_new)

    pt = page_table.reshape(-1)
    as3 = lambda a: a[:, None, :]
    osb8, oc8, sel_s = _sample1(
        pt, as3(q_sb_s), as3(q_perm),
        cache_sb_k[0].reshape(n_pool, PAGE, SB_WIDTH), cache_sb_v[0].reshape(n_pool, PAGE, SB_WIDTH),
        cache_cmp_k[0].reshape(n_pool, PAGE // CMP_STRIDE, CMP_STRIDE * KV_WIDTH),
        cache_cmp_v[0].reshape(n_pool, PAGE // CMP_STRIDE, CMP_STRIDE * KV_WIDTH), consts, n_c, n_sel)
    os8, ow8 = _sample2(
        pt, as3(q_perm), cache_sel_k[0].reshape(n_pool, PAGE, KV_WIDTH), cache_sel_v[0].reshape(n_pool, PAGE, KV_WIDTH),
        sel_s, (as3(ks_s), as3(vs_s), as3(kw_s), as3(vw_s)),
        state_win_k[0].reshape(n_seq, w_buf, KV_WIDTH), state_win_v[0].reshape(n_seq, w_buf, KV_WIDTH), consts)

    hsel = np.arange(SB_HEADS)
    o_sb_s = osb8.reshape(n_seq, SB_HEADS, SB_HEADS, HEAD_DIM)[:, hsel, hsel].reshape(n_seq, SB_WIDTH)
    pick = lambda o8: (o8.reshape(n_seq, NSA_HEADS, NSA_KV, HEAD_DIM)[:, hsel, hsel // NSA_GROUP]
                       .reshape(n_seq, NSA_WIDTH))
    h_s, hn_s, comb_s = _merge(xs, o_sb_s, pick(oc8), pick(os8), pick(ow8), mw, n_seq)
    y_sample = _moe(h_s, hn_s, comb_s, p_sample[0, :, 0], ew, n_seq)[:, None]

    hd = lambda a, n: a.reshape((1, a.shape[0]) + (() if a.ndim == 2 else ()) + (n, HEAD_DIM))
    pr = lambda a, n: a.reshape(1, 1, seq, n, HEAD_DIM)
    sm = lambda a, n: a.reshape(1, n_seq, 1, n, HEAD_DIM)
    win_p = lambda a: a[seq - WINDOW:].reshape(1, 1, WINDOW, NSA_KV, HEAD_DIM)
    win_s = lambda st, new: jnp.concatenate([st[0], new.reshape(n_seq, 1, NSA_KV, HEAD_DIM)], axis=1)[None, :, 1:]
    return (y_prompt, y_sample,
            pr(k_sb, SB_HEADS), pr(v_sb, SB_HEADS), pr(k_c, NSA_KV), pr(v_c, NSA_KV), pr(k_s, NSA_KV),
            pr(v_s, NSA_KV), win_p(k_w), win_p(v_w),
            sm(k_sb_s, SB_HEADS), sm(v_sb_s, SB_HEADS), sm(kc_s, NSA_KV), sm(vc_s, NSA_KV), sm(ks_s, NSA_KV),
            sm(vs_s, NSA_KV), win_s(state_win_k, kw_s), win_s(state_win_v, vw_s))
```

```python
import functools
import math

import numpy as np
import jax
import jax.numpy as jnp
from jax import lax
from jax.experimental import pallas as pl
from jax.experimental.pallas import tpu as pltpu

F32 = jnp.float32
BF16 = jnp.bfloat16
HIGHEST = lax.Precision.HIGHEST

D_MODEL = 1024
HEAD_DIM = 64
SB_HEADS = 8
NSA_HEADS = 8
NSA_KV = 2
NSA_GROUP = NSA_HEADS // NSA_KV
SB_WIDTH = SB_HEADS * HEAD_DIM
NSA_WIDTH = NSA_HEADS * HEAD_DIM
KV_WIDTH = NSA_KV * HEAD_DIM
CMP_BLOCK = 32
CMP_STRIDE = 16
CMP_HIDDEN = 256
SEL_BLOCK = 64
N_SELECT = 16
N_FORCED = 3
WINDOW = 512
N_BUCKETS = 32
MAX_DISTANCE = 128
N_GROUPS = 4
EXPERTS_PER_GROUP = 8
N_EXPERTS = N_GROUPS * EXPERTS_PER_GROUP
EXPERT_FF = 256
PLE_DIM = 256
PAGE = 128
RMS_EPS = 1e-6
NEG = -1e30
LANES = 128
EXP_UNDERFLOW = -104.0
VMEM_LIMIT = 56 * 1024 * 1024

SEL_PAD = 3 * LANES
WIN_PAD = WINDOW


def _cparams(sem):
    return pltpu.CompilerParams(dimension_semantics=sem, vmem_limit_bytes=VMEM_LIMIT)


def _rms(x, g):
    return x * lax.rsqrt(jnp.mean(x * x, axis=-1, keepdims=True) + RMS_EPS) * g


def _dot(a, b):
    return jnp.dot(a, b, preferred_element_type=F32)


def _dot_hp(a, b):
    return jnp.dot(a, b, preferred_element_type=F32, precision=HIGHEST)


def _dot_nt(a, b):
    return lax.dot_general(a, b, (((1,), (1,)), ((), ())), preferred_element_type=F32)


def _dot_tn_hp(a, b):
    return lax.dot_general(a, b, (((0,), (0,)), ((), ())), preferred_element_type=F32,
                           precision=HIGHEST)


def _r16(a):
    return a.astype(BF16).astype(F32)


def _t5_bucket_np(n):
    n = np.maximum(n, 0)
    max_exact = N_BUCKETS // 2
    nf = np.maximum(n, 1).astype(np.float32)
    large = max_exact + (np.log(nf / np.float32(max_exact)) / np.float32(math.log(MAX_DISTANCE / max_exact))
                         * np.float32(N_BUCKETS - max_exact)).astype(np.int32)
    large = np.minimum(large, N_BUCKETS - 1)
    return np.where(n < max_exact, n, large)


def _proj_kernel(x_ref, g_ref, w_ref, wt_ref,
                 qsb_ref, ksb_ref, vsb_ref, vsbb_ref, qn_ref,
                 kc_ref, vc_ref, ks_ref, vs_ref, kw_ref, vw_ref,
                 vsa_ref, vwa_ref, ksbt_ref, kst_ref, kwt_ref):
    xb = _rms(x_ref[...], g_ref[...]).astype(BF16)

    def cols(a, b):
        return _dot(xb, w_ref[:, a:b])

    u = cols(0, 512)
    for h in range(SB_HEADS):
        qsb_ref[h] = u[:, h * HEAD_DIM:(h + 1) * HEAD_DIM].astype(BF16)
    ksb_ref[...] = cols(512, 1024)
    v = cols(1024, 1536)
    vsb_ref[...] = v
    vsbb_ref[...] = v.astype(BF16)
    u = cols(1536, 2048)
    for h in range(NSA_HEADS):
        qn_ref[h] = u[:, h * HEAD_DIM:(h + 1) * HEAD_DIM].astype(BF16)
    for i, r in enumerate((kc_ref, vc_ref, ks_ref, vs_ref, kw_ref, vw_ref)):
        r[...] = cols(2048 + i * KV_WIDTH, 2048 + (i + 1) * KV_WIDTH)
    ones_col = lax.broadcasted_iota(jnp.int32, (xb.shape[0], LANES), 1) == HEAD_DIM
    base = 2048 + 6 * KV_WIDTH
    for kv in range(NSA_KV):
        a = cols(base + kv * LANES, base + (kv + 1) * LANES)
        vsa_ref[kv] = jnp.where(ones_col, 1.0, a).astype(BF16)
        a = cols(base + (NSA_KV + kv) * LANES, base + (NSA_KV + kv + 1) * LANES)
        vwa_ref[kv] = jnp.where(ones_col, 1.0, a).astype(BF16)
    ksbt_ref[...] = _dot_nt(wt_ref[0:512, :], xb).astype(BF16)
    kst_ref[...] = _dot_nt(wt_ref[512:640, :], xb).astype(BF16)
    kwt_ref[...] = _dot_nt(wt_ref[640:768, :], xb).astype(BF16)


def _proj(x, g, w_all, w_t, tm):
    s = x.shape[0]
    nw = w_all.shape[1]
    row = lambda n: pl.BlockSpec((tm, n), lambda i: (i, 0))
    head = pl.BlockSpec((SB_HEADS, tm, HEAD_DIM), lambda i: (0, i, 0))
    aug = pl.BlockSpec((NSA_KV, tm, LANES), lambda i: (0, i, 0))
    tr = lambda n: pl.BlockSpec((n, tm), lambda i: (0, i))
    sd = jax.ShapeDtypeStruct
    out_shape = (
        sd((SB_HEADS, s, HEAD_DIM), BF16), sd((s, SB_WIDTH), F32), sd((s, SB_WIDTH), F32),
        sd((s, SB_WIDTH), BF16), sd((NSA_HEADS, s, HEAD_DIM), BF16),
        *(sd((s, KV_WIDTH), F32) for _ in range(6)),
        sd((NSA_KV, s, LANES), BF16), sd((NSA_KV, s, LANES), BF16),
        sd((SB_WIDTH, s), BF16), sd((KV_WIDTH, s), BF16), sd((KV_WIDTH, s), BF16))
    out_specs = (head, row(SB_WIDTH), row(SB_WIDTH), row(SB_WIDTH), head,
                 *(row(KV_WIDTH) for _ in range(6)), aug, aug,
                 tr(SB_WIDTH), tr(KV_WIDTH), tr(KV_WIDTH))
    return pl.pallas_call(
        _proj_kernel,
        grid=(s // tm,),
        in_specs=[row(D_MODEL), pl.BlockSpec((1, D_MODEL), lambda i: (0, 0)),
                  pl.BlockSpec((D_MODEL, nw), lambda i: (0, 0)),
                  pl.BlockSpec((w_t.shape[0], D_MODEL), lambda i: (0, 0))],
        out_specs=out_specs,
        out_shape=out_shape,
        compiler_params=_cparams(("parallel",)),
        name="proj_prompt",
    )(x, g, w_all, w_t)


def _proj_s_kernel(x_ref, g_ref, w_ref, o_ref):
    o_ref[...] = _dot(_rms(x_ref[...], g_ref[...]).astype(BF16), w_ref[...])


def _proj_s(x, g, w):
    b, n = x.shape[0], w.shape[1]
    return pl.pallas_call(
        _proj_s_kernel,
        out_shape=jax.ShapeDtypeStruct((b, n), F32),
        compiler_params=pltpu.CompilerParams(vmem_limit_bytes=VMEM_LIMIT),
        name="proj_sample",
    )(x, g, w)


def _sb_kernel(q_ref, kt_ref, v_ref, tri_ref, o_ref, *, tq):
    qi = pl.program_id(1)
    tri = tri_ref[...]
    row = lax.broadcasted_iota(jnp.int32, (tq, tq), 0)
    col = lax.broadcasted_iota(jnp.int32, (tq, tq), 1)
    below = col < row
    res = []
    for hh in range(2):
        q = q_ref[hh]

        def tile(j, run, diagonal, hh=hh, q=q):
            ks = pl.multiple_of(j * tq, tq)
            kt = kt_ref[hh * HEAD_DIM:(hh + 1) * HEAD_DIM, pl.ds(ks, tq)]
            z = _dot(q, kt)
            t = jnp.log(1.0 + jnp.exp(-jnp.abs(z)))
            log_keep = -(jnp.maximum(z, 0.0) + t)
            if diagonal:
                log_keep = jnp.where(below, log_keep, 0.0)
            after = _dot(log_keep.astype(BF16), tri)
            a = jnp.exp(jnp.minimum(z, 0.0) - t + after + run)
            if diagonal:
                a = jnp.where(below, a, 0.0)
            pv = _dot(a.astype(BF16), v_ref[pl.ds(ks, tq), :])
            return pv, run + after[:, 0:1] + log_keep[:, 0:1]

        pv0, run0 = tile(qi, jnp.zeros((tq, 1), F32), True)

        def cond(c):
            return jnp.logical_and(c[0] <= qi, c[3])

        def body(c):
            jj, acc, run, _ = c
            pv, run = tile(qi - jj, run, False)
            return jj + 1, acc + pv, run, jnp.max(run) > EXP_UNDERFLOW

        c = lax.while_loop(cond, body, (jnp.int32(1), pv0, run0, jnp.max(run0) > EXP_UNDERFLOW))
        res.append(c[1])
    lane = lax.broadcasted_iota(jnp.int32, (tq, LANES), 1)
    o_ref[...] = jnp.where(lane < HEAD_DIM, res[0], res[1])


def _sb_prompt(q_h, k_t, v_b, tq):
    s = k_t.shape[1]
    tri = jnp.asarray(np.tril(np.ones((tq, tq), np.float32), -1), BF16)
    return pl.pallas_call(
        functools.partial(_sb_kernel, tq=tq),
        grid=(SB_HEADS // 2, s // tq),
        in_specs=[pl.BlockSpec((2, tq, HEAD_DIM), lambda p, i: (p, i, 0)),
                  pl.BlockSpec((LANES, s), lambda p, i: (p, 0)),
                  pl.BlockSpec((s, LANES), lambda p, i: (0, p)),
                  pl.BlockSpec((tq, tq), lambda p, i: (0, 0))],
        out_specs=pl.BlockSpec((tq, LANES), lambda p, i: (i, p)),
        out_shape=jax.ShapeDtypeStruct((s, SB_WIDTH), F32),
        compiler_params=_cparams(("parallel", "parallel")),
        name="sb_prompt",
    )(q_h, k_t, v_b, tri)


def _compress_rows(x, pe_a, pe_b, w_a, w_b, b1, w2):
    m = x.shape[0]
    a = _dot((x + pe_a).astype(BF16), w_a)
    b = _dot((x + pe_b).astype(BF16), w_b)
    b = pltpu.roll(b, m - 1, 0)
    return jax.nn.gelu(a + b + b1)


def _compress_kernel(x_ref, pea_ref, peb_ref, wa_ref, wb_ref, b1_ref, w2_ref, w2t_ref, w2a_ref,
                     cmp_t_ref, cmp_a_ref):
    hid = _compress_rows(x_ref[...], pea_ref[...], peb_ref[...], wa_ref[...], wb_ref[...],
                         b1_ref[...], None).astype(BF16)
    cmp_t_ref[...] = _dot_nt(w2t_ref[...], hid).astype(BF16)
    for kv in range(NSA_KV):
        cmp_a_ref[kv] = _dot(hid, w2a_ref[kv]).astype(BF16)


def _compress_prompt(x2, cw):
    m = x2.shape[1]
    lead = lambda *shape: pl.BlockSpec((None,) + shape, lambda i: (i,) + (0,) * len(shape))
    return pl.pallas_call(
        _compress_kernel,
        grid=(2,),
        in_specs=[lead(m, 16 * KV_WIDTH), lead(1, 16 * KV_WIDTH), lead(1, 16 * KV_WIDTH),
                  lead(16 * KV_WIDTH, 2 * CMP_HIDDEN), lead(16 * KV_WIDTH, 2 * CMP_HIDDEN),
                  lead(1, 2 * CMP_HIDDEN), lead(2 * CMP_HIDDEN, KV_WIDTH),
                  lead(KV_WIDTH, 2 * CMP_HIDDEN), lead(NSA_KV, 2 * CMP_HIDDEN, LANES)],
        out_specs=(lead(KV_WIDTH, m), lead(NSA_KV, m, LANES)),
        out_shape=(jax.ShapeDtypeStruct((2, KV_WIDTH, m), BF16),
                   jax.ShapeDtypeStruct((2, NSA_KV, m, LANES), BF16)),
        compiler_params=_cparams(("parallel",)),
        name="compress_prompt",
    )(x2, cw["pe_a"], cw["pe_b"], cw["w_a"], cw["w_b"], cw["b1"], cw["w2"], cw["w2t"], cw["w2a"])


def _compress_weights(pe, w1, b1, w2):
    half = CMP_BLOCK // 2
    eye = jnp.eye(NSA_KV, dtype=F32)
    w1r = w1.reshape(2, half, HEAD_DIM, CMP_HIDDEN)

    def big(wh):
        return jnp.einsum("rdf,pk->rpdkf", wh, eye).reshape(half * KV_WIDTH, NSA_KV * CMP_HIDDEN)

    def perow(p):
        return jnp.broadcast_to(p[:, None, :], (half, NSA_KV, HEAD_DIM)).reshape(1, half * KV_WIDTH)

    w2bd = jnp.einsum("fd,pk->pfkd", w2, eye).reshape(NSA_KV * CMP_HIDDEN, KV_WIDTH)
    w2a = jnp.stack([jnp.concatenate(
        [w2bd[:, kv * HEAD_DIM:(kv + 1) * HEAD_DIM], jnp.zeros((NSA_KV * CMP_HIDDEN, LANES - HEAD_DIM), F32)],
        axis=1) for kv in range(NSA_KV)])
    return dict(pe_a=perow(pe[:half]), pe_b=perow(pe[half:]),
                w_a=big(w1r[0]).astype(BF16), w_b=big(w1r[1]).astype(BF16),
                b1=jnp.tile(b1, NSA_KV)[None], w2=w2bd.astype(BF16), w2t=w2bd.T.astype(BF16),
                w2a=w2a.astype(BF16))


def _top_select(vals, idx, forced, axis, rounds):
    sel = forced
    for _ in range(rounds):
        mx = jnp.max(vals, axis=axis, keepdims=True)
        first = jnp.min(jnp.where(vals == mx, idx, jnp.int32(1 << 30)), axis=axis, keepdims=True)
        hit = jnp.logical_and(idx == first, mx > 0.5 * NEG)
        sel = jnp.logical_or(sel, hit)
        vals = jnp.where(hit, NEG, vals)
    return sel


def _pack_heads(o, tq):
    lane = lax.broadcasted_iota(jnp.int32, (tq, LANES), 1)
    o4 = o.reshape(NSA_GROUP, tq, LANES)
    halves = []
    for a in range(NSA_GROUP // 2):
        halves.append(jnp.where(lane < HEAD_DIM, o4[2 * a], pltpu.roll(o4[2 * a + 1], HEAD_DIM, 1)))
    return halves


def _nsa_cmp_kernel(q_ref, ltab_ref, kt_ref, v_ref, ov_ref, o_ref, sel_ref, *, tq, n_blk):
    qi = pl.program_id(1)
    rows = NSA_GROUP * tq
    n_c = kt_ref.shape[1]
    q4 = q_ref[...].reshape(rows, HEAD_DIM)
    r = lax.broadcasted_iota(jnp.int32, (HEAD_DIM, n_c), 0)
    c = lax.broadcasted_iota(jnp.int32, (HEAD_DIM, n_c), 1)
    place = jnp.where(jnp.logical_and(r < 32, c == (tq // CMP_STRIDE) * qi + (r & 15) - 9), 1.0, 0.0)
    s = _dot(q4, kt_ref[...]) + _dot(ltab_ref[...], place.astype(BF16))
    qpos = qi * tq + (lax.broadcasted_iota(jnp.int32, (rows, 1), 0) & (tq - 1))
    c_max = (qpos - (CMP_BLOCK - 1)) >> 4
    mask = lax.broadcasted_iota(jnp.int32, (rows, n_c), 1) <= c_max
    s = jnp.where(mask, s, NEG)
    m = jnp.max(s, axis=-1, keepdims=True)
    e = jnp.where(mask, jnp.exp(s - m), 0.0)
    p = e / jnp.maximum(jnp.sum(e, axis=-1, keepdims=True), 1e-30)
    o = _dot(p.astype(BF16), v_ref[...])
    h0, h1 = _pack_heads(o, tq)
    o_ref[:, 0:LANES] = h0
    o_ref[:, LANES:2 * LANES] = h1
    p4 = p.reshape(NSA_GROUP, tq, n_c)
    pg = p4[0] + p4[1] + p4[2] + p4[3]
    imp = _dot(pg.astype(BF16), ov_ref[...])
    j = lax.broadcasted_iota(jnp.int32, (tq, n_blk), 1)
    q_blk = (qi * tq + lax.broadcasted_iota(jnp.int32, (tq, 1), 0)) >> 6
    forced = jnp.logical_or(j == 0, jnp.logical_and(j >= q_blk - 1, j <= q_blk))
    cand = jnp.logical_and(j >= 1, j <= q_blk - 2)
    sel = _top_select(jnp.where(cand, imp, NEG), j, forced, 1, N_SELECT - N_FORCED)
    sel_ref[...] = jnp.where(sel, 1.0, 0.0).astype(BF16)


def _nsa_cmp_prompt(qn_h, ltab, cmp_kt, cmp_va, overlap, tq):
    s = qn_h.shape[1]
    n_c = cmp_kt.shape[2]
    n_blk = overlap.shape[1]
    return pl.pallas_call(
        functools.partial(_nsa_cmp_kernel, tq=tq, n_blk=n_blk),
        grid=(NSA_KV, s // tq),
        in_specs=[pl.BlockSpec((NSA_GROUP, tq, HEAD_DIM), lambda k, i: (k, i, 0)),
                  pl.BlockSpec((None, NSA_GROUP * tq, HEAD_DIM), lambda k, i: (k, 0, 0)),
                  pl.BlockSpec((None, HEAD_DIM, n_c), lambda k, i: (k, 0, 0)),
                  pl.BlockSpec((None, n_c, LANES), lambda k, i: (k, 0, 0)),
                  pl.BlockSpec((n_c, n_blk), lambda k, i: (0, 0))],
        out_specs=(pl.BlockSpec((tq, 2 * LANES), lambda k, i: (i, k)),
                   pl.BlockSpec((None, tq, n_blk), lambda k, i: (k, i, 0))),
        out_shape=(jax.ShapeDtypeStruct((s, NSA_WIDTH), F32),
                   jax.ShapeDtypeStruct((NSA_KV, s, n_blk), BF16)),
        compiler_params=_cparams(("parallel", "parallel")),
        name="nsa_cmp_prompt",
    )(qn_h, ltab, cmp_kt, cmp_va, overlap)


def _nsa_sel_kernel(q_ref, kt_ref, v_ref, sel_ref, pn_ref, o_ref, *, tq, n_blk):
    qi = pl.program_id(1)
    rows = NSA_GROUP * tq
    tk_far = SEL_PAD + LANES
    q4 = q_ref[...].reshape(rows, HEAD_DIM)
    selm = sel_ref[...]

    def tile(start, tk, bias, m, acc):
        ps = pl.multiple_of(start + SEL_PAD, LANES)
        jb = lax.broadcasted_iota(jnp.int32, (n_blk, tk), 0)
        lk = lax.broadcasted_iota(jnp.int32, (n_blk, tk), 1)
        expand = jnp.where(jb == ((start + lk) >> 6), 1.0, 0.0).astype(BF16)
        drop = (_dot(selm, expand) - 1.0) * (-NEG)
        s = _dot(q4, kt_ref[:, pl.ds(ps, tk)])
        if bias is not None:
            s = s + bias
        s3 = s.reshape(NSA_GROUP, tq, tk) + drop[None]
        m_new = jnp.maximum(m, jnp.max(s3, axis=-1, keepdims=True))
        p = jnp.exp(s3 - m_new).reshape(rows, tk)
        alpha = jnp.exp(m - m_new).reshape(rows, 1)
        acc = alpha * acc + _dot(p.astype(BF16), v_ref[pl.ds(ps, tk), :])
        return m_new, acc

    near = (qi - 1) * tq
    m0 = jnp.full((NSA_GROUP, tq, 1), NEG, F32)
    m, acc = tile(near, 2 * tq, pn_ref[...], m0, jnp.zeros((rows, LANES), F32))

    def body(t, c):
        return tile(near - tk_far * (t + 1), tk_far, None, *c)

    m, acc = lax.fori_loop(0, (near + tk_far - 1) // tk_far, body, (m, acc))
    o = acc / jnp.maximum(acc[:, HEAD_DIM:HEAD_DIM + 1], 1e-30)
    h0, h1 = _pack_heads(o, tq)
    o_ref[:, 0:LANES] = h0
    o_ref[:, LANES:2 * LANES] = h1


def _nsa_sel_prompt(qn_h, ks_t, vs_a, sel, p_near, tq):
    s = qn_h.shape[1]
    sp = ks_t.shape[2]
    n_blk = sel.shape[2]
    return pl.pallas_call(
        functools.partial(_nsa_sel_kernel, tq=tq, n_blk=n_blk),
        grid=(NSA_KV, s // tq),
        in_specs=[pl.BlockSpec((NSA_GROUP, tq, HEAD_DIM), lambda k, i: (k, i, 0)),
                  pl.BlockSpec((None, HEAD_DIM, sp), lambda k, i: (k, 0, 0)),
                  pl.BlockSpec((None, sp, LANES), lambda k, i: (k, 0, 0)),
                  pl.BlockSpec((None, tq, n_blk), lambda k, i: (k, i, 0)),
                  pl.BlockSpec((None, NSA_GROUP * tq, 2 * tq), lambda k, i: (k, 0, 0))],
        out_specs=pl.BlockSpec((tq, 2 * LANES), lambda k, i: (i, k)),
        out_shape=jax.ShapeDtypeStruct((s, NSA_WIDTH), F32),
        compiler_params=_cparams(("parallel", "parallel")),
        name="nsa_sel_prompt",
    )(qn_h, ks_t, vs_a, sel, p_near)


def _nsa_win_kernel(q_ref, kt_ref, v_ref, pw_ref, o_ref, *, tq):
    qi = pl.program_id(1)
    rows = NSA_GROUP * tq
    span = WINDOW + tq
    q4 = q_ref[...].reshape(rows, HEAD_DIM)
    ps = pl.multiple_of(qi * tq, LANES)
    s = _dot(q4, kt_ref[:, pl.ds(ps, span)]) + pw_ref[...]
    real = lax.broadcasted_iota(jnp.int32, (rows, span), 1) >= WIN_PAD - qi * tq
    s = jnp.where(real, s, NEG)
    m = jnp.max(s, axis=-1, keepdims=True)
    e = jnp.where(real, jnp.exp(s - m), 0.0)
    acc = _dot(e.astype(BF16), v_ref[pl.ds(ps, span), :])
    o = acc / jnp.maximum(acc[:, HEAD_DIM:HEAD_DIM + 1], 1e-30)
    h0, h1 = _pack_heads(o, tq)
    o_ref[:, 0:LANES] = h0
    o_ref[:, LANES:2 * LANES] = h1


def _nsa_win_prompt(qn_h, kw_t, vw_a, p_win, tq):
    s = qn_h.shape[1]
    sp = kw_t.shape[2]
    return pl.pallas_call(
        functools.partial(_nsa_win_kernel, tq=tq),
        grid=(NSA_KV, s // tq),
        in_specs=[pl.BlockSpec((NSA_GROUP, tq, HEAD_DIM), lambda k, i: (k, i, 0)),
                  pl.BlockSpec((None, HEAD_DIM, sp), lambda k, i: (k, 0, 0)),
                  pl.BlockSpec((None, sp, LANES), lambda k, i: (k, 0, 0)),
                  pl.BlockSpec((None, NSA_GROUP * tq, WINDOW + tq), lambda k, i: (k, 0, 0))],
        out_specs=pl.BlockSpec((tq, 2 * LANES), lambda k, i: (i, k)),
        out_shape=jax.ShapeDtypeStruct((s, NSA_WIDTH), F32),
        compiler_params=_cparams(("parallel", "parallel")),
        name="nsa_win_prompt",
    )(qn_h, kw_t, vw_a, p_win)


def _route(logits):
    lane = lax.broadcasted_iota(jnp.int32, logits.shape, 1)
    big = jnp.int32(1 << 30)
    is_g = jnp.logical_and(lane >= N_EXPERTS, lane < N_EXPERTS + N_GROUPS)
    gl = jnp.where(is_g, logits, NEG)
    gmax = jnp.max(gl, axis=-1, keepdims=True)
    gidx = jnp.min(jnp.where(gl == gmax, lane - N_EXPERTS, big), axis=-1, keepdims=True)
    g_w = 1.0 / jnp.sum(jnp.where(is_g, jnp.exp(gl - gmax), 0.0), axis=-1, keepdims=True)
    in_g = jnp.logical_and(lane < N_EXPERTS, (lane >> 3) == gidx)
    el = jnp.where(in_g, logits, NEG)
    emax = jnp.max(el, axis=-1, keepdims=True)
    ee = jnp.where(in_g, jnp.exp(el - emax), 0.0)
    prob = jnp.where(in_g, ee / jnp.sum(ee, axis=-1, keepdims=True), -1.0)
    p1 = jnp.max(prob, axis=-1, keepdims=True)
    i1 = jnp.min(jnp.where(prob == p1, lane, big), axis=-1, keepdims=True)
    prob2 = jnp.where(lane == i1, -1.0, prob)
    p2 = jnp.max(prob2, axis=-1, keepdims=True)
    i2 = jnp.min(jnp.where(prob2 == p2, lane, big), axis=-1, keepdims=True)
    tot = p1 + p2
    return jnp.where(lane == i1, g_w * (p1 / tot), jnp.where(lane == i2, g_w * (p2 / tot), 0.0))


def _merge_kernel(x_ref, osb_ref, oc_ref, os_ref, ow_ref, gmix_ref, wg_ref, eg_ref, wbs_ref, wbn_ref,
                  wo_ref, gffn_ref, wr_ref, h_ref, hn_ref, comb_ref):
    mm = lambda a, w_ref: _dot(a.astype(BF16), w_ref[...])
    x = x_ref[...]
    u = mm(_rms(x, gmix_ref[...]), wg_ref)
    g = jax.nn.sigmoid(u[:, 0:LANES])
    g_hi = _r16(g)
    g_mid = _r16(g - g_hi)
    g_br = mm(g_hi, eg_ref) + mm(g_mid, eg_ref) + mm(g - g_hi - g_mid, eg_ref)
    o_nsa = (g_br[:, 0:NSA_WIDTH] * oc_ref[...] + g_br[:, NSA_WIDTH:2 * NSA_WIDTH] * os_ref[...]
             + g_br[:, 2 * NSA_WIDTH:3 * NSA_WIDTH] * ow_ref[...])
    merged = (jax.nn.sigmoid(u[:, LANES:LANES + D_MODEL]) * mm(osb_ref[...], wbs_ref)
              + jax.nn.sigmoid(u[:, LANES + D_MODEL:LANES + 2 * D_MODEL]) * mm(o_nsa, wbn_ref))
    h = x + mm(merged, wo_ref)
    hn = _rms(h, gffn_ref[...])
    h_ref[...] = h
    hn_ref[...] = hn.astype(BF16)
    comb_ref[...] = _route(mm(hn, wr_ref))


def _merge(x, o_sb, o_c, o_s, o_w, mw, tm):
    n = x.shape[0]
    row = lambda c: pl.BlockSpec((tm, c), lambda i: (i, 0))
    full = lambda a: pl.BlockSpec(a.shape, lambda i: (0,) * a.ndim)
    ws = (mw["g_mix"], mw["w_gate"], mw["e_g"], mw["w_b_sb"], mw["w_b_nsa"], mw["w_o"], mw["g_ffn"], mw["w_r"])
    return pl.pallas_call(
        _merge_kernel,
        grid=(n // tm,),
        in_specs=[row(D_MODEL), row(SB_WIDTH), row(NSA_WIDTH), row(NSA_WIDTH), row(NSA_WIDTH)]
                 + [full(a) for a in ws],
        out_specs=(row(D_MODEL), row(D_MODEL), row(LANES)),
        out_shape=(jax.ShapeDtypeStruct((n, D_MODEL), F32), jax.ShapeDtypeStruct((n, D_MODEL), BF16),
                   jax.ShapeDtypeStruct((n, LANES), F32)),
        compiler_params=_cparams(("parallel",)),
        name="merge",
    )(x, o_sb, o_c, o_s, o_w, *ws)


def _moe_kernel(h_ref, hn_ref, comb_ref, pe_ref, weg_ref, weu_ref, wed_ref, gple_ref, wpg_ref, wpe_ref,
                gfin_ref, y_ref, acc_ref):
    e = pl.program_id(1)

    @pl.when(e == 0)
    def _():
        acc_ref[...] = jnp.zeros_like(acc_ref)

    hn = hn_ref[...]
    lane = lax.broadcasted_iota(jnp.int32, comb_ref.shape, 1)
    ce = jnp.sum(jnp.where(lane == e, comb_ref[...], 0.0), axis=-1, keepdims=True)
    hid = jax.nn.silu(_dot(hn, weg_ref[...])) * _dot(hn, weu_ref[...]) * ce
    acc_ref[...] += _dot(hid.astype(BF16), wed_ref[...])

    @pl.when(e == N_EXPERTS - 1)
    def _():
        h = h_ref[...] + acc_ref[...]
        gate = jax.nn.sigmoid(_dot(_rms(h, gple_ref[...]).astype(BF16), wpg_ref[...]))
        h = h + gate * _dot(pe_ref[...].astype(BF16), wpe_ref[...])
        y_ref[...] = _rms(h, gfin_ref[...])


def _moe(h, hn, comb, p_emb, ew, tm):
    n = h.shape[0]
    row = lambda c: pl.BlockSpec((tm, c), lambda i, e: (i, 0))
    full = lambda a: pl.BlockSpec(a.shape, lambda i, e: (0,) * a.ndim)
    exp = lambda a, b: pl.BlockSpec((None, a, b), lambda i, e: (e, 0, 0))
    return pl.pallas_call(
        _moe_kernel,
        grid=(n // tm, N_EXPERTS),
        in_specs=[row(D_MODEL), row(D_MODEL), row(LANES), row(PLE_DIM),
                  exp(D_MODEL, EXPERT_FF), exp(D_MODEL, EXPERT_FF), exp(EXPERT_FF, D_MODEL),
                  full(ew["g_ple"]), full(ew["w_pg"]), full(ew["w_pe"]), full(ew["g_fin"])],
        out_specs=row(D_MODEL),
        out_shape=jax.ShapeDtypeStruct((n, D_MODEL), F32),
        scratch_shapes=[pltpu.VMEM((tm, D_MODEL), F32)],
        compiler_params=_cparams(("parallel", "arbitrary")),
        name="moe_ple",
    )(h, hn, comb, p_emb, ew["w_eg"], ew["w_eu"], ew["w_ed"], ew["g_ple"], ew["w_pg"], ew["w_pe"], ew["g_fin"])


def _sample1_kernel(pt_ref, qsb_ref, qp_ref, sbk_ref, sbv_ref, cck_ref, ccv_ref,
                    ssb_ref, tri_ref, snsa_ref, pea_ref, peb_ref, wa_ref, wb_ref, b1_ref, w2_ref,
                    w2a_ref, bc_ref, ovt_ref, gs_ref,
                    osb_ref, oc_ref, sel_ref,
                    x2k_ref, x2v_ref, acc_ref, run_ref, *, n_pages, n_c, n_sel):
    del pt_ref
    p = pl.program_id(1)
    page = n_pages - 1 - p

    @pl.when(p == 0)
    def _():
        acc_ref[...] = jnp.zeros_like(acc_ref)
        run_ref[...] = jnp.zeros_like(run_ref)

    z = _dot_hp(_r16(sbk_ref[...]) * _r16(qsb_ref[...]), ssb_ref[...])
    t = jnp.log(1.0 + jnp.exp(-jnp.abs(z)))
    log_keep = -(jnp.maximum(z, 0.0) + t)
    after = _dot_hp(tri_ref[...], log_keep)
    a = jnp.exp(jnp.minimum(z, 0.0) - t + after + run_ref[...])
    acc_ref[...] += _dot_tn_hp(_r16(a), _r16(sbv_ref[...]))
    run_ref[...] += jnp.sum(log_keep, axis=0, keepdims=True)

    rows = pl.ds(pl.multiple_of(page * 8, 8), 8)
    x2k_ref[rows, :] = cck_ref[...]
    x2v_ref[rows, :] = ccv_ref[...]

    @pl.when(p == n_pages - 1)
    def _():
        osb_ref[...] = acc_ref[0:8, :]
        cmp = []
        for i, x2 in enumerate((x2k_ref, x2v_ref)):
            hid = _compress_rows(x2[...], pea_ref[i], peb_ref[i], wa_ref[i], wb_ref[i], b1_ref[i], None)
            cmp.append(hid.astype(BF16))
        cmp_k = _dot(cmp[0], w2_ref[0])
        cmp_v = _dot(cmp[1], w2_ref[1])
        m_rows = cmp_k.shape[0]
        k4 = jnp.concatenate([_r16(cmp_k)] * NSA_GROUP, axis=1)
        s = _dot_hp(k4 * _r16(qp_ref[...]), snsa_ref[...]) + bc_ref[...]
        valid = lax.broadcasted_iota(jnp.int32, s.shape, 0) < n_c
        s = jnp.where(valid, s, NEG)
        m = jnp.max(s, axis=0, keepdims=True)
        e = jnp.where(valid, jnp.exp(s - m), 0.0)
        pc = e / jnp.maximum(jnp.sum(e, axis=0, keepdims=True), 1e-30)
        oc_ref[...] = _dot_tn_hp(_r16(pc), _r16(cmp_v))[0:8, :]
        imp = _dot_hp(ovt_ref[...], _r16(_dot_hp(pc, gs_ref[...])))
        j = lax.broadcasted_iota(jnp.int32, imp.shape, 0)
        forced = jnp.logical_or(j == 0, jnp.logical_and(j >= n_sel - 2, j < n_sel))
        cand = jnp.logical_and(j >= 1, j < n_sel - 2)
        sel = _top_select(jnp.where(cand, imp, NEG), j, forced, 0, N_SELECT - N_FORCED)
        sel_ref[...] = jnp.where(sel, 1.0, 0.0).reshape(sel_ref.shape)


def _sample1(pt, q_sb, q_perm, sbk, sbv, cck, ccv, consts, n_c, n_sel):
    b = q_sb.shape[0]
    n_pages = pt.shape[0] // b
    m_rows = n_pages * 8
    n_blk_pad = consts["ov_t"].shape[0]
    page_map = lambda bb, p, pt_ref: (pt_ref[bb * n_pages + n_pages - 1 - p], 0, 0)
    seq = lambda *shape: pl.BlockSpec((None,) + shape, lambda bb, p, pt_ref: (bb,) + (0,) * len(shape))
    full = lambda a: pl.BlockSpec(a.shape, lambda bb, p, pt_ref: (0,) * a.ndim)
    cs = [consts[k] for k in ("s_sb", "tri", "s_nsa", "pe_a", "pe_b", "w_a", "w_b", "b1", "w2", "w2a",
                              "b_cmp", "ov_t", "g_sum")]
    grid_spec = pltpu.PrefetchScalarGridSpec(
        num_scalar_prefetch=1,
        grid=(b, n_pages),
        in_specs=[seq(1, SB_WIDTH), seq(1, NSA_WIDTH),
                  pl.BlockSpec((None, PAGE, SB_WIDTH), page_map),
                  pl.BlockSpec((None, PAGE, SB_WIDTH), page_map),
                  pl.BlockSpec((None, 8, 16 * KV_WIDTH), page_map),
                  pl.BlockSpec((None, 8, 16 * KV_WIDTH), page_map)] + [full(a) for a in cs],
        out_specs=(seq(8, SB_WIDTH), seq(8, LANES), seq(n_blk_pad // 8, 8, LANES)),
        scratch_shapes=[pltpu.VMEM((m_rows, 16 * KV_WIDTH), F32), pltpu.VMEM((m_rows, 16 * KV_WIDTH), F32),
                        pltpu.VMEM((LANES, SB_WIDTH), F32), pltpu.VMEM((1, LANES), F32)])
    return pl.pallas_call(
        functools.partial(_sample1_kernel, n_pages=n_pages, n_c=n_c, n_sel=n_sel),
        grid_spec=grid_spec,
        out_shape=(jax.ShapeDtypeStruct((b, 8, SB_WIDTH), F32), jax.ShapeDtypeStruct((b, 8, LANES), F32),
                   jax.ShapeDtypeStruct((b, n_blk_pad // 8, 8, LANES), F32)),
        compiler_params=_cparams(("parallel", "arbitrary")),
        name="sample_pass1",
    )(pt, q_sb, q_perm, sbk, sbv, cck, ccv, *cs)


def _sample2_kernel(pt_ref, qp_ref, sk_ref, sv_ref, sel_ref, blast_ref, snsa_ref,
                    nks_ref, nvs_ref, nkw_ref, nvw_ref, wk_ref, wv_ref, bw_ref, bnew_ref,
                    os_ref, ow_ref, s_scr, v_scr, *, n_pages):
    del pt_ref
    p = pl.program_id(1)
    q_row = _r16(qp_ref[...])

    def scores(k):
        k4 = jnp.concatenate([_r16(k)] * NSA_GROUP, axis=1)
        return _dot_hp(k4 * q_row, snsa_ref[...])

    tile = sel_ref[p >> 2]
    sub = lax.broadcasted_iota(jnp.int32, tile.shape, 0)
    r0 = jnp.sum(jnp.where(sub == 2 * (p & 3), tile, 0.0), axis=0, keepdims=True)
    r1 = jnp.sum(jnp.where(sub == 2 * (p & 3) + 1, tile, 0.0), axis=0, keepdims=True)
    tok = lax.broadcasted_iota(jnp.int32, (PAGE, LANES), 0)
    keep = jnp.where(tok < SEL_BLOCK, r0, r1) > 0.5
    bias = jnp.where(p == n_pages - 1, blast_ref[...], 0.0)
    s_scr[p] = jnp.where(keep, scores(sk_ref[...]) + bias, NEG)
    v_scr[pl.ds(pl.multiple_of(p * PAGE, PAGE), PAGE), :] = sv_ref[...].astype(BF16)

    @pl.when(p == n_pages - 1)
    def _():
        first = tok == 0

        def attend(s_past, v_past, k_new_ref, v_new_ref):
            s_new = jnp.where(first, scores(jnp.where(first, k_new_ref[...], 0.0)) + bnew_ref[...], NEG)
            m = jnp.maximum(jnp.max(s_past, axis=0, keepdims=True), jnp.max(s_new, axis=0, keepdims=True))
            e_past = jnp.where(s_past > 0.5 * NEG, jnp.exp(s_past - m), 0.0)
            e_new = jnp.where(first, jnp.exp(s_new - m), 0.0)
            l = jnp.maximum(jnp.sum(e_past, axis=0, keepdims=True) + jnp.sum(e_new, axis=0, keepdims=True), 1e-30)
            o = _dot((e_past / l).T.astype(BF16), v_past)
            o = o + _r16(e_new / l).T[:, 0:1] * _r16(v_new_ref[...])
            return o[0:8, :]

        os_ref[...] = attend(s_scr[...].reshape(n_pages * PAGE, LANES), v_scr[...], nks_ref, nvs_ref)
        in_win = lax.broadcasted_iota(jnp.int32, (wk_ref.shape[0], LANES), 0) >= 1
        s_w = jnp.where(in_win, scores(wk_ref[...]) + bw_ref[...], NEG)
        ow_ref[...] = attend(s_w, wv_ref[...].astype(BF16), nkw_ref, nvw_ref)


def _sample2(pt, q_perm, sk, sv, sel, new_rows, win_k, win_v, consts):
    b = q_perm.shape[0]
    n_pages = pt.shape[0] // b
    w_rows = win_k.shape[1]
    page_map = lambda bb, p, pt_ref: (pt_ref[bb * n_pages + p], 0, 0)
    seq = lambda *shape: pl.BlockSpec((None,) + shape, lambda bb, p, pt_ref: (bb,) + (0,) * len(shape))
    full = lambda a: pl.BlockSpec(a.shape, lambda bb, p, pt_ref: (0,) * a.ndim)
    grid_spec = pltpu.PrefetchScalarGridSpec(
        num_scalar_prefetch=1,
        grid=(b, n_pages),
        in_specs=[seq(1, NSA_WIDTH),
                  pl.BlockSpec((None, PAGE, KV_WIDTH), page_map),
                  pl.BlockSpec((None, PAGE, KV_WIDTH), page_map),
                  seq(*sel.shape[1:]), full(consts["b_last"]), full(consts["s_nsa"]),
                  seq(1, KV_WIDTH), seq(1, KV_WIDTH), seq(1, KV_WIDTH), seq(1, KV_WIDTH),
                  seq(w_rows, KV_WIDTH), seq(w_rows, KV_WIDTH), full(consts["b_win"]), full(consts["b_new"])],
        out_specs=(seq(8, LANES), seq(8, LANES)),
        scratch_shapes=[pltpu.VMEM((n_pages, PAGE, LANES), F32), pltpu.VMEM((n_pages * PAGE, KV_WIDTH), BF16)])
    return pl.pallas_call(
        functools.partial(_sample2_kernel, n_pages=n_pages),
        grid_spec=grid_spec,
        out_shape=(jax.ShapeDtypeStruct((b, 8, LANES), F32), jax.ShapeDtypeStruct((b, 8, LANES), F32)),
        compiler_params=_cparams(("parallel", "arbitrary")),
        name="sample_pass2",
    )(pt, q_perm, sk, sv, sel, consts["b_last"], consts["s_nsa"], *new_rows, win_k, win_v,
      consts["b_win"], consts["b_new"])


def _sample_kernel(pt_ref, qsb_ref, qnt_ref, sbk_ref, sbv_ref, cck_ref, ccv_ref, slk_ref, slv_ref,
                   tri_ref, pe_ref, w1a_ref, w1b_ref, b1_ref, w2_ref, bc_ref, bl_ref, bw_ref, bn_ref,
                   ovt_ref, gs_ref, nks_ref, nvs_ref, nkw_ref, nvw_ref, wk_ref, wv_ref,
                   osb_ref, oc_ref, os_ref, ow_ref,
                   x2_scr, s_scr, v_scr, acc_scr, run_scr, *, n_pages, n_c, n_sel):
    del pt_ref
    p = pl.program_id(1)
    page = n_pages - 1 - p
    m_rows = n_pages * (PAGE // CMP_STRIDE)
    lane = lax.broadcasted_iota(jnp.int32, (PAGE, LANES), 1)

    @pl.when(p == 0)
    def _():
        acc_scr[...] = jnp.zeros_like(acc_scr)
        run_scr[...] = jnp.zeros_like(run_scr)

    def nsa_scores(k0, k1):
        return _dot(k0.astype(BF16), qnt_ref[0]) + _dot(k1.astype(BF16), qnt_ref[1])

    def per_kv(ref, rows):
        return [ref[pl.ds(kv, rows, stride=NSA_KV), :] for kv in range(NSA_KV)]

    run = run_scr[...]
    alive = jnp.max(jnp.where(lane[0:1] < SB_HEADS, run, NEG)) > EXP_UNDERFLOW

    @pl.when(alive)
    def _():
        qh = _r16(qsb_ref[...])
        z = jnp.zeros((PAGE, LANES), F32)
        for h in range(SB_HEADS):
            kh = _r16(sbk_ref[pl.ds(h, PAGE, stride=SB_HEADS), :])
            z = jnp.where(lane == h, jnp.sum(kh * qh[h:h + 1, :], axis=1, keepdims=True), z)
        t = jnp.log(1.0 + jnp.exp(-jnp.abs(z)))
        log_keep = -(jnp.maximum(z, 0.0) + t)
        after = _dot_hp(tri_ref[...], log_keep)
        a = _r16(jnp.exp(jnp.minimum(z, 0.0) - t + after + run))
        for h in range(SB_HEADS):
            vh = _r16(sbv_ref[pl.ds(h, PAGE, stride=SB_HEADS), :])
            acc_scr[h:h + 1, :] += jnp.sum(a[:, h:h + 1] * vh, axis=0, keepdims=True)
        run_scr[...] = run + jnp.sum(log_keep, axis=0, keepdims=True)

    crow = pl.multiple_of(page * (PAGE // CMP_STRIDE), 8)
    for i, ref in enumerate((cck_ref, ccv_ref)):
        for kv in range(NSA_KV):
            for r2 in range(CMP_STRIDE // 2):
                pair = [ref[pl.ds(NSA_KV * r + kv, PAGE // CMP_STRIDE, stride=NSA_KV * CMP_STRIDE), :]
                        for r in (2 * r2, 2 * r2 + 1)]
                x2_scr[i, pl.ds(kv * m_rows + crow, PAGE // CMP_STRIDE), r2 * LANES:(r2 + 1) * LANES] = (
                    jnp.concatenate(pair, axis=1))

    s_scr[page] = nsa_scores(*per_kv(slk_ref, PAGE))
    trow = pl.multiple_of(page * PAGE, PAGE)
    for kv, v in enumerate(per_kv(slv_ref, PAGE)):
        v_scr[kv, pl.ds(trow, PAGE), :] = v.astype(BF16)

    @pl.when(p == n_pages - 1)
    def _():
        osb_ref[...] = acc_scr[...]
        row8 = lax.broadcasted_iota(jnp.int32, (8, HEAD_DIM), 0)
        rr = lax.broadcasted_iota(jnp.int32, (LANES, LANES), 0)
        cc = lax.broadcasted_iota(jnp.int32, (LANES, LANES), 1)

        def weighted(prob, v0, v1):
            pt = prob.T.astype(BF16)
            return _dot(pt, v0.astype(BF16)), _dot(pt, v1.astype(BF16))

        def pick(o0, o1):
            return jnp.where(row8 < NSA_GROUP, o0[0:8], o1[0:8])

        cmp = []
        for i in range(2):
            x2 = x2_scr[i]
            a = _dot((x2 + pe_ref[i, 0:1, :]).astype(BF16), w1a_ref[i])
            b = _dot((x2 + pe_ref[i, 1:2, :]).astype(BF16), w1b_ref[i])
            b = pltpu.roll(b, NSA_KV * m_rows - 1, 0)
            cmp.append(_dot(jax.nn.gelu(a + b + b1_ref[i]).astype(BF16), w2_ref[i]))
        ck, cv = cmp
        s = nsa_scores(ck[0:m_rows], ck[m_rows:]) + bc_ref[...]
        valid = lax.broadcasted_iota(jnp.int32, s.shape, 0) < n_c
        s = jnp.where(valid, s, NEG)
        e = jnp.where(valid, jnp.exp(s - jnp.max(s, axis=0, keepdims=True)), 0.0)
        pc = e / jnp.maximum(jnp.sum(e, axis=0, keepdims=True), 1e-30)
        oc_ref[...] = pick(*weighted(pc, cv[0:m_rows], cv[m_rows:]))
        imp = _dot_hp(ovt_ref[...], _r16(_dot_hp(pc, gs_ref[...])))
        j = lax.broadcasted_iota(jnp.int32, imp.shape, 0)
        forced = jnp.logical_or(j == 0, jnp.logical_and(j >= n_sel - 2, j < n_sel))
        cand = jnp.logical_and(j >= 1, j < n_sel - 2)
        sel = _top_select(jnp.where(cand, imp, NEG), j, forced, 0, N_SELECT - N_FORCED)

        def attend(s_past, v0, v1, k_new_ref, v_new_ref):
            kn = [jnp.broadcast_to(k_new_ref[kv:kv + 1, :], (16, HEAD_DIM)) for kv in range(NSA_KV)]
            s_new = nsa_scores(*kn)[0:1] + bn_ref[...]
            m = jnp.maximum(jnp.max(s_past, axis=0, keepdims=True), s_new)
            e_past = jnp.where(s_past > 0.5 * NEG, jnp.exp(s_past - m), 0.0)
            e_new = jnp.exp(s_new - m)
            l = jnp.maximum(jnp.sum(e_past, axis=0, keepdims=True) + e_new, 1e-30)
            o0, o1 = weighted(e_past / l, v0, v1)
            p_new = jnp.sum(jnp.where(rr == cc, _r16(e_new / l), 0.0), axis=1, keepdims=True)
            vn = _r16(v_new_ref[...])
            return pick(o0 + p_new * vn[0:1], o1 + p_new * vn[1:2])

        pg = lax.broadcasted_iota(jnp.int32, (n_pages, PAGE, LANES), 0)
        s_all = s_scr[...] + jnp.where(pg == n_pages - 1, bl_ref[...][None], 0.0)
        blocks = 2 * n_pages
        keep = jnp.broadcast_to(jnp.where(sel, 1.0, 0.0)[0:blocks][:, None, :], (blocks, SEL_BLOCK, LANES))
        s_all = jnp.where(keep.reshape(n_pages * PAGE, LANES) > 0.5, s_all.reshape(n_pages * PAGE, LANES), NEG)
        os_ref[...] = attend(s_all, v_scr[0], v_scr[1], nks_ref, nvs_ref)
        w_rows = wk_ref.shape[0] // NSA_KV
        in_win = lax.broadcasted_iota(jnp.int32, (w_rows, LANES), 0) >= 1
        s_w = jnp.where(in_win, nsa_scores(*per_kv(wk_ref, w_rows)) + bw_ref[...], NEG)
        ow_ref[...] = attend(s_w, *per_kv(wv_ref, w_rows), nkw_ref, nvw_ref)


def _sample(pt, q_sb, qn_t, caches, new_rows, win_k, win_v, consts, n_c, n_sel):
    b = q_sb.shape[0]
    n_pages = pt.shape[0] // b
    m_rows = n_pages * (PAGE // CMP_STRIDE)
    page_map = lambda bb, p, pt_ref: (pt_ref[bb * n_pages + n_pages - 1 - p], 0, 0)
    seq = lambda *shape: pl.BlockSpec((None,) + shape, lambda bb, p, pt_ref: (bb,) + (0,) * len(shape))
    full = lambda a: pl.BlockSpec(a.shape, lambda bb, p, pt_ref: (0,) * a.ndim)
    paged = lambda a: pl.BlockSpec((None,) + a.shape[1:], page_map)
    cs = [consts[k] for k in ("tri", "pe", "w1a", "w1b", "b1", "w2", "b_cmp", "b_last", "b_win", "b_new",
                              "ov_t", "g_sum")]
    out = jax.ShapeDtypeStruct((b, 8, HEAD_DIM), F32)
    grid_spec = pltpu.PrefetchScalarGridSpec(
        num_scalar_prefetch=1,
        grid=(b, n_pages),
        in_specs=[seq(SB_HEADS, HEAD_DIM), seq(NSA_KV, HEAD_DIM, LANES)] + [paged(a) for a in caches]
                 + [full(a) for a in cs] + [seq(NSA_KV, HEAD_DIM)] * 4
                 + [seq(*win_k.shape[1:]), seq(*win_v.shape[1:])],
        out_specs=(seq(8, HEAD_DIM),) * 4,
        scratch_shapes=[pltpu.VMEM((2, NSA_KV * m_rows, CMP_STRIDE * HEAD_DIM), F32),
                        pltpu.VMEM((n_pages, PAGE, LANES), F32),
                        pltpu.VMEM((NSA_KV, n_pages * PAGE, HEAD_DIM), BF16),
                        pltpu.VMEM((SB_HEADS, HEAD_DIM), F32), pltpu.VMEM((1, LANES), F32)])
    return pl.pallas_call(
        functools.partial(_sample_kernel, n_pages=n_pages, n_c=n_c, n_sel=n_sel),
        grid_spec=grid_spec,
        out_shape=(out,) * 4,
        compiler_params=_cparams(("parallel", "arbitrary")),
        name="sample_mix",
    )(pt, q_sb, qn_t, *caches, *cs, *new_rows, win_k, win_v)


def _onehot_rows(delta, dist):
    oh = (np.arange(MAX_DISTANCE)[:, None] == np.asarray(dist)[None, :]).astype(np.float32)
    return jnp.dot(delta, jnp.asarray(oh), precision=HIGHEST)


def _bias_delta(table):
    bucket = _t5_bucket_np(np.arange(MAX_DISTANCE))
    oh = (np.arange(N_BUCKETS)[:, None] == bucket[None, :]).astype(np.float32)
    return jnp.dot((table - table[N_BUCKETS - 1][None]).T, jnp.asarray(oh), precision=HIGHEST)


def _toeplitz(f, n, w):
    lf = n + w - 1
    x = jnp.broadcast_to(f[:, None, :], (f.shape[0], n, lf))
    x = jnp.pad(x, ((0, 0), (0, 0), (0, 1))).reshape(f.shape[0], n * (lf + 1))[:, :n * lf]
    return x.reshape(f.shape[0], n, lf)[:, :, n - 1:n - 1 + w]


def _pad_lanes(a, n=LANES):
    return jnp.concatenate([a, jnp.zeros(a.shape[:-1] + (n - a.shape[-1],), a.dtype)], axis=-1)


def kernel(x_prompt, x_sample, cache_sb_k, cache_sb_v, cache_cmp_k, cache_cmp_v, cache_sel_k, cache_sel_v, state_win_k, state_win_v, page_table, p_prompt, p_sample, norm_mix, w_in, cmp_pe_k, cmp_w1_k, cmp_b1_k, cmp_w2_k, cmp_pe_v, cmp_w1_v, cmp_b1_v, cmp_w2_v, rel_bias_table, w_branch_sb, w_branch_nsa, w_out, norm_ffn, w_router_group, w_router_expert, w_exp_gate, w_exp_up, w_exp_down, norm_ple, w_ple_gate, w_ple, norm_final):
    assert w_in.shape[0] == 1 and x_prompt.shape[0] == 1 and x_sample.shape[1] == 1
    seq = x_prompt.shape[1]
    n_seq = x_sample.shape[0]
    n_pool = cache_sb_k.shape[1]
    n_pages = page_table.shape[1]
    past = n_pages * PAGE
    w_buf = state_win_k.shape[2]
    tq = 128
    assert seq % 512 == 0 and w_buf == WINDOW and past >= WINDOW and seq >= WINDOW

    scale = HEAD_DIM ** -0.5
    w = w_in[0]
    o_g = 2048 + 6 * KV_WIDTH
    w_q_sb, w_k_sb, w_v_sb, w_q_n = (w[:, i * 512:(i + 1) * 512] for i in range(4))
    w_kv6 = w[:, 2048:o_g]
    w_vs, w_vw = w[:, 2048 + 3 * KV_WIDTH:2048 + 4 * KV_WIDTH], w[:, 2048 + 5 * KV_WIDTH:o_g]
    w_ks, w_kw = w[:, 2048 + 2 * KV_WIDTH:2048 + 3 * KV_WIDTH], w[:, 2048 + 4 * KV_WIDTH:2048 + 5 * KV_WIDTH]
    aug = lambda wv, kv: _pad_lanes(wv[:, kv * HEAD_DIM:(kv + 1) * HEAD_DIM])
    w_all = jnp.concatenate([w_q_sb * scale, w_k_sb, w_v_sb, w_q_n * scale, w_kv6,
                             aug(w_vs, 0), aug(w_vs, 1), aug(w_vw, 0), aug(w_vw, 1)], axis=1).astype(BF16)
    w_t = jnp.concatenate([w_k_sb.T, w_ks.T, w_kw.T], axis=0).astype(BF16)
    g_mix = norm_mix[0][None]
    w_gate_f = jnp.concatenate([_pad_lanes(w[:, o_g:o_g + 3 * NSA_HEADS]), w[:, o_g + 3 * NSA_HEADS:]], axis=1)
    e_g = np.zeros((LANES, 3 * NSA_WIDTH), np.float32)
    for h in range(NSA_HEADS):
        for br in range(3):
            e_g[h * 3 + br, br * NSA_WIDTH + h * HEAD_DIM:br * NSA_WIDTH + (h + 1) * HEAD_DIM] = 1.0
    w_r = jnp.concatenate([w_router_expert[0], w_router_group[0],
                           jnp.zeros((D_MODEL, LANES - N_EXPERTS - N_GROUPS), F32)], axis=1)
    mw = dict(g_mix=g_mix, w_gate=w_gate_f.astype(BF16), e_g=jnp.asarray(e_g, BF16),
              w_b_sb=w_branch_sb[0].astype(BF16), w_b_nsa=w_branch_nsa[0].astype(BF16),
              w_o=w_out[0].astype(BF16), g_ffn=norm_ffn[0][None], w_r=w_r.astype(BF16))
    ew = dict(w_eg=w_exp_gate[0].astype(BF16), w_eu=w_exp_up[0].astype(BF16), w_ed=w_exp_down[0].astype(BF16),
              g_ple=norm_ple[0][None], w_pg=w_ple_gate[0].astype(BF16), w_pe=w_ple[0].astype(BF16),
              g_fin=norm_final[None])
    cw_k = _compress_weights(cmp_pe_k[0], cmp_w1_k[0], cmp_b1_k[0], cmp_w2_k[0])
    cw_v = _compress_weights(cmp_pe_v[0], cmp_w1_v[0], cmp_b1_v[0], cmp_w2_v[0])
    cw = {k: jnp.stack([cw_k[k], cw_v[k]]) for k in cw_k}

    delta = _bias_delta(rel_bias_table)
    rows = NSA_GROUP * tq
    by_kv = lambda a: a.reshape((NSA_KV, rows) + a.shape[2:])
    d_c = (np.arange(tq)[:, None] - CMP_STRIDE * (np.arange(16)[None] - 9) - (CMP_BLOCK - 1)).reshape(-1)
    l_tab = by_kv(_onehot_rows(delta, d_c).reshape(NSA_HEADS, tq, 16))
    l_hi = l_tab.astype(BF16)
    l_lo = (l_tab - l_hi.astype(F32)).astype(BF16)
    l_tab = jnp.concatenate([l_hi, l_lo, jnp.zeros((NSA_KV, rows, HEAD_DIM - 32), BF16)], axis=-1)
    d_n = 2 * tq - 1 - np.arange(3 * tq - 1)
    f_n = _onehot_rows(delta, d_n) + jnp.asarray(np.where(d_n < 0, NEG, 0.0), F32)[None]
    p_near = by_kv(_toeplitz(f_n, tq, 2 * tq))
    d_w = WINDOW + tq - 1 - np.arange(WINDOW + 2 * tq - 1)
    f_w = _onehot_rows(delta, d_w) + jnp.asarray(np.where((d_w < 0) | (d_w >= WINDOW), NEG, 0.0), F32)[None]
    p_win = by_kv(_toeplitz(f_w, tq, WINDOW + tq))

    xp = x_prompt[0]
    (q_sb_h, k_sb, v_sb, v_sb_b, q_n_h, k_c, v_c, k_s, v_s, k_w, v_w, vs_a, vw_a,
     k_sb_t, k_s_t, k_w_t) = _proj(xp, g_mix, w_all, w_t, 512)
    o_sb = _sb_prompt(q_sb_h, k_sb_t, v_sb_b, 256)

    x2 = jnp.stack([k_c, v_c]).reshape(2, seq // CMP_STRIDE, CMP_STRIDE * KV_WIDTH)
    cmp_t, cmp_a = _compress_prompt(x2, cw)
    n_cp = seq // CMP_STRIDE
    n_blk = seq // SEL_BLOCK
    c_start = np.arange(n_cp)[:, None] * CMP_STRIDE
    b_start = np.arange(n_blk)[None] * SEL_BLOCK
    overlap = ((c_start < b_start + SEL_BLOCK) & (c_start + CMP_BLOCK > b_start)
               & (np.arange(n_cp)[:, None] < n_cp - 1))
    overlap = jnp.asarray(overlap.astype(np.float32), BF16)
    o_c, sel = _nsa_cmp_prompt(q_n_h, l_tab, cmp_t[0].reshape(NSA_KV, HEAD_DIM, n_cp), cmp_a[1], overlap, tq)

    front = lambda a, n, axis: jnp.pad(a, [(n, 0) if ax == axis else (0, 0) for ax in range(a.ndim)])
    o_s = _nsa_sel_prompt(q_n_h, front(k_s_t.reshape(NSA_KV, HEAD_DIM, seq), SEL_PAD, 2),
                          front(vs_a, SEL_PAD, 1), sel, p_near, tq)
    o_w = _nsa_win_prompt(q_n_h, front(k_w_t.reshape(NSA_KV, HEAD_DIM, seq), WIN_PAD, 2),
                          front(vw_a, WIN_PAD, 1), p_win, tq)
    h_p, hn_p, comb_p = _merge(xp, o_sb, o_c, o_s, o_w, mw, 512)
    y_prompt = _moe(h_p, hn_p, comb_p, p_prompt[0, 0], ew, min(1024, seq))[None]

    w_s = jnp.concatenate([w_q_sb * scale, w_k_sb, w_v_sb, w_q_n * scale, w_kv6], axis=1).astype(BF16)
    xs = x_sample[:, 0]
    u = _proj_s(xs, g_mix, w_s)
    q_sb_s, k_sb_s, v_sb_s, q_n_s = (u[:, i * 512:(i + 1) * 512] for i in range(4))
    kc_s, vc_s, ks_s, vs_s, kw_s, vw_s = (u[:, 2048 + i * KV_WIDTH:2048 + (i + 1) * KV_WIDTH] for i in range(6))
    q_t = jnp.swapaxes(q_n_s.reshape(n_seq, NSA_KV, NSA_GROUP, HEAD_DIM), 2, 3)
    qn_t = jnp.stack([jnp.pad(q_t[:, kv], ((0, 0), (0, 0), (kv * NSA_GROUP, LANES - (kv + 1) * NSA_GROUP)))
                      for kv in range(NSA_KV)], axis=1).astype(BF16)

    n_c = (past + 1 - CMP_BLOCK) // CMP_STRIDE + 1
    n_sel = -(-(past + 1) // SEL_BLOCK)
    m_rows = past // CMP_STRIDE
    n_blk_pad = -(-n_sel // 32) * 32
    tri_s = np.triu(np.ones((PAGE, PAGE), np.float32), 1)
    ov_t = np.zeros((n_blk_pad, m_rows), np.float32)
    cs = np.arange(n_c) * CMP_STRIDE
    for jb in range(n_sel):
        ov_t[jb, :n_c] = (cs < (jb + 1) * SEL_BLOCK) & (cs + CMP_BLOCK > jb * SEL_BLOCK)
    g_sum = np.zeros((LANES, LANES), np.float32)
    g_sum[:NSA_HEADS, :NSA_HEADS] = (np.arange(NSA_HEADS)[:, None] // NSA_GROUP
                                     == np.arange(NSA_HEADS)[None] // NSA_GROUP)
    rows_of = lambda dist: _pad_lanes(_onehot_rows(delta, dist).T)
    b_cmp = rows_of(past - (np.arange(m_rows) * CMP_STRIDE + CMP_BLOCK - 1))
    b_last = rows_of(PAGE - np.arange(PAGE))
    b_win = rows_of(w_buf - np.arange(w_buf))
    b_new = rows_of(np.zeros((1,), np.int64))
    half = CMP_BLOCK // 2
    cmp_p = ((cmp_pe_k[0], cmp_w1_k[0], cmp_b1_k[0], cmp_w2_k[0]), (cmp_pe_v[0], cmp_w1_v[0], cmp_b1_v[0], cmp_w2_v[0]))
    stack = lambda f, dt=F32: jnp.stack([f(*c) for c in cmp_p]).astype(dt)
    consts = dict(tri=jnp.asarray(tri_s),
                  pe=stack(lambda pe, w1, b1, w2: pe.reshape(2, half * HEAD_DIM)),
                  w1a=stack(lambda pe, w1, b1, w2: w1[:half * HEAD_DIM], BF16),
                  w1b=stack(lambda pe, w1, b1, w2: w1[half * HEAD_DIM:], BF16),
                  b1=stack(lambda pe, w1, b1, w2: b1[None]), w2=stack(lambda pe, w1, b1, w2: w2, BF16),
                  b_cmp=b_cmp, b_last=b_last, b_win=b_win, b_new=b_new,
                  ov_t=jnp.asarray(ov_t), g_sum=jnp.asarray(g_sum))

    native = lambda c: c[0].reshape(c.shape[1], c.shape[2] * c.shape[3], HEAD_DIM)
    caches = [native(c) for c in (cache_sb_k, cache_sb_v, cache_cmp_k, cache_cmp_v, cache_sel_k, cache_sel_v)]
    new_rows = [a.reshape(n_seq, NSA_KV, HEAD_DIM) for a in (ks_s, vs_s, kw_s, vw_s)]
    o4 = _sample(page_table.reshape(-1), q_sb_s.reshape(n_seq, SB_HEADS, HEAD_DIM), qn_t, caches, new_rows,
                 native(state_win_k), native(state_win_v), consts, n_c, n_sel)
    o_sb_s, o_c_s, o_s_s, o_w_s = (o.reshape(n_seq, SB_WIDTH) for o in o4)
    h_s, hn_s, comb_s = _merge(xs, o_sb_s, o_c_s, o_s_s, o_w_s, mw, n_seq)
    y_sample = _moe(h_s, hn_s, comb_s, p_sample[0, :, 0], ew, n_seq)[:, None]

    hd = lambda a, n: a.reshape((1, a.shape[0]) + (() if a.ndim == 2 else ()) + (n, HEAD_DIM))
    pr = lambda a, n: a.reshape(1, 1, seq, n, HEAD_DIM)
    sm = lambda a, n: a.reshape(1, n_seq, 1, n, HEAD_DIM)
    win_p = lambda a: a[seq - WINDOW:].reshape(1, 1, WINDOW, NSA_KV, HEAD_DIM)
    win_s = lambda st, new: jnp.concatenate([st[0], new.reshape(n_seq, 1, NSA_KV, HEAD_DIM)], axis=1)[None, :, 1:]
    return (y_prompt, y_sample,
            pr(k_sb, SB_HEADS), pr(v_sb, SB_HEADS), pr(k_c, NSA_KV), pr(v_c, NSA_KV), pr(k_s, NSA_KV),
            pr(v_s, NSA_KV), win_p(k_w), win_p(v_w),
            sm(k_sb_s, SB_HEADS), sm(v_sb_s, SB_HEADS), sm(kc_s, NSA_KV), sm(vc_s, NSA_KV), sm(ks_s, NSA_KV),
            sm(vs_s, NSA_KV), win_s(state_win_k, kw_s), win_s(state_win_v, vw_s))
```

```python
import functools
import math

import numpy as np
import jax
import jax.numpy as jnp
from jax import lax
from jax.experimental import pallas as pl
from jax.experimental.pallas import tpu as pltpu

F32 = jnp.float32
BF16 = jnp.bfloat16
HIGHEST = lax.Precision.HIGHEST

D_MODEL = 1024
HEAD_DIM = 64
SB_HEADS = 8
NSA_HEADS = 8
NSA_KV = 2
NSA_GROUP = NSA_HEADS // NSA_KV
SB_WIDTH = SB_HEADS * HEAD_DIM
NSA_WIDTH = NSA_HEADS * HEAD_DIM
KV_WIDTH = NSA_KV * HEAD_DIM
CMP_BLOCK = 32
CMP_STRIDE = 16
CMP_HIDDEN = 256
SEL_BLOCK = 64
N_SELECT = 16
N_FORCED = 3
WINDOW = 512
N_BUCKETS = 32
MAX_DISTANCE = 128
N_GROUPS = 4
EXPERTS_PER_GROUP = 8
N_EXPERTS = N_GROUPS * EXPERTS_PER_GROUP
EXPERT_FF = 256
PLE_DIM = 256
PAGE = 128
RMS_EPS = 1e-6
NEG = -1e30
LANES = 128
EXP_UNDERFLOW = -104.0
VMEM_LIMIT = 56 * 1024 * 1024

SEL_PAD = 3 * LANES
WIN_PAD = WINDOW


def _cparams(sem):
    return pltpu.CompilerParams(dimension_semantics=sem, vmem_limit_bytes=VMEM_LIMIT)


def _rms(x, g):
    return x * lax.rsqrt(jnp.mean(x * x, axis=-1, keepdims=True) + RMS_EPS) * g


def _dot(a, b):
    return jnp.dot(a, b, preferred_element_type=F32)


def _dot_hp(a, b):
    return jnp.dot(a, b, preferred_element_type=F32, precision=HIGHEST)


def _dot_nt(a, b):
    return lax.dot_general(a, b, (((1,), (1,)), ((), ())), preferred_element_type=F32)


def _dot_tn_hp(a, b):
    return lax.dot_general(a, b, (((0,), (0,)), ((), ())), preferred_element_type=F32,
                           precision=HIGHEST)


def _r16(a):
    return a.astype(BF16).astype(F32)


def _t5_bucket_np(n):
    n = np.maximum(n, 0)
    max_exact = N_BUCKETS // 2
    nf = np.maximum(n, 1).astype(np.float32)
    large = max_exact + (np.log(nf / np.float32(max_exact)) / np.float32(math.log(MAX_DISTANCE / max_exact))
                         * np.float32(N_BUCKETS - max_exact)).astype(np.int32)
    large = np.minimum(large, N_BUCKETS - 1)
    return np.where(n < max_exact, n, large)


def _proj_kernel(x_ref, g_ref, w_ref, wt_ref,
                 qsb_ref, ksb_ref, vsb_ref, vsbb_ref, qn_ref,
                 kc_ref, vc_ref, ks_ref, vs_ref, kw_ref, vw_ref,
                 vsa_ref, vwa_ref, ksbt_ref, kst_ref, kwt_ref):
    xb = _rms(x_ref[...], g_ref[...]).astype(BF16)

    def cols(a, b):
        return _dot(xb, w_ref[:, a:b])

    u = cols(0, 512)
    for h in range(SB_HEADS):
        qsb_ref[h] = u[:, h * HEAD_DIM:(h + 1) * HEAD_DIM].astype(BF16)
    ksb_ref[...] = cols(512, 1024)
    v = cols(1024, 1536)
    vsb_ref[...] = v
    vsbb_ref[...] = v.astype(BF16)
    u = cols(1536, 2048)
    for h in range(NSA_HEADS):
        qn_ref[h] = u[:, h * HEAD_DIM:(h + 1) * HEAD_DIM].astype(BF16)
    for i, r in enumerate((kc_ref, vc_ref, ks_ref, vs_ref, kw_ref, vw_ref)):
        r[...] = cols(2048 + i * KV_WIDTH, 2048 + (i + 1) * KV_WIDTH)
    ones_col = lax.broadcasted_iota(jnp.int32, (xb.shape[0], LANES), 1) == HEAD_DIM
    base = 2048 + 6 * KV_WIDTH
    for kv in range(NSA_KV):
        a = cols(base + kv * LANES, base + (kv + 1) * LANES)
        vsa_ref[kv] = jnp.where(ones_col, 1.0, a).astype(BF16)
        a = cols(base + (NSA_KV + kv) * LANES, base + (NSA_KV + kv + 1) * LANES)
        vwa_ref[kv] = jnp.where(ones_col, 1.0, a).astype(BF16)
    ksbt_ref[...] = _dot_nt(wt_ref[0:512, :], xb).astype(BF16)
    kst_ref[...] = _dot_nt(wt_ref[512:640, :], xb).astype(BF16)
    kwt_ref[...] = _dot_nt(wt_ref[640:768, :], xb).astype(BF16)


def _proj(x, g, w_all, w_t, tm):
    s = x.shape[0]
    nw = w_all.shape[1]
    row = lambda n: pl.BlockSpec((tm, n), lambda i: (i, 0))
    head = pl.BlockSpec((SB_HEADS, tm, HEAD_DIM), lambda i: (0, i, 0))
    aug = pl.BlockSpec((NSA_KV, tm, LANES), lambda i: (0, i, 0))
    tr = lambda n: pl.BlockSpec((n, tm), lambda i: (0, i))
    sd = jax.ShapeDtypeStruct
    out_shape = (
        sd((SB_HEADS, s, HEAD_DIM), BF16), sd((s, SB_WIDTH), F32), sd((s, SB_WIDTH), F32),
        sd((s, SB_WIDTH), BF16), sd((NSA_HEADS, s, HEAD_DIM), BF16),
        *(sd((s, KV_WIDTH), F32) for _ in range(6)),
        sd((NSA_KV, s, LANES), BF16), sd((NSA_KV, s, LANES), BF16),
        sd((SB_WIDTH, s), BF16), sd((KV_WIDTH, s), BF16), sd((KV_WIDTH, s), BF16))
    out_specs = (head, row(SB_WIDTH), row(SB_WIDTH), row(SB_WIDTH), head,
                 *(row(KV_WIDTH) for _ in range(6)), aug, aug,
                 tr(SB_WIDTH), tr(KV_WIDTH), tr(KV_WIDTH))
    return pl.pallas_call(
        _proj_kernel,
        grid=(s // tm,),
        in_specs=[row(D_MODEL), pl.BlockSpec((1, D_MODEL), lambda i: (0, 0)),
                  pl.BlockSpec((D_MODEL, nw), lambda i: (0, 0)),
                  pl.BlockSpec((w_t.shape[0], D_MODEL), lambda i: (0, 0))],
        out_specs=out_specs,
        out_shape=out_shape,
        compiler_params=_cparams(("parallel",)),
        name="proj_prompt",
    )(x, g, w_all, w_t)


def _proj_s_kernel(x_ref, g_ref, w_ref, o_ref):
    o_ref[...] = _dot(_rms(x_ref[...], g_ref[...]).astype(BF16), w_ref[...])


def _proj_s(x, g, w):
    b, n = x.shape[0], w.shape[1]
    return pl.pallas_call(
        _proj_s_kernel,
        out_shape=jax.ShapeDtypeStruct((b, n), F32),
        compiler_params=pltpu.CompilerParams(vmem_limit_bytes=VMEM_LIMIT),
        name="proj_sample",
    )(x, g, w)


def _sb_kernel(q_ref, kt_ref, v_ref, tri_ref, o_ref, *, tq):
    qi = pl.program_id(1)
    tri = tri_ref[...]
    row = lax.broadcasted_iota(jnp.int32, (tq, tq), 0)
    col = lax.broadcasted_iota(jnp.int32, (tq, tq), 1)
    below = col < row
    res = []
    for hh in range(2):
        q = q_ref[hh]

        def tile(j, run, diagonal, hh=hh, q=q):
            ks = pl.multiple_of(j * tq, tq)
            kt = kt_ref[hh * HEAD_DIM:(hh + 1) * HEAD_DIM, pl.ds(ks, tq)]
            z = _dot(q, kt)
            t = jnp.log(1.0 + jnp.exp(-jnp.abs(z)))
            log_keep = -(jnp.maximum(z, 0.0) + t)
            if diagonal:
                log_keep = jnp.where(below, log_keep, 0.0)
            after = _dot(log_keep.astype(BF16), tri)
            a = jnp.exp(jnp.minimum(z, 0.0) - t + after + run)
            if diagonal:
                a = jnp.where(below, a, 0.0)
            pv = _dot(a.astype(BF16), v_ref[pl.ds(ks, tq), :])
            return pv, run + after[:, 0:1] + log_keep[:, 0:1]

        pv0, run0 = tile(qi, jnp.zeros((tq, 1), F32), True)

        def cond(c):
            return jnp.logical_and(c[0] <= qi, c[3])

        def body(c):
            jj, acc, run, _ = c
            pv, run = tile(qi - jj, run, False)
            return jj + 1, acc + pv, run, jnp.max(run) > EXP_UNDERFLOW

        c = lax.while_loop(cond, body, (jnp.int32(1), pv0, run0, jnp.max(run0) > EXP_UNDERFLOW))
        res.append(c[1])
    lane = lax.broadcasted_iota(jnp.int32, (tq, LANES), 1)
    o_ref[...] = jnp.where(lane < HEAD_DIM, res[0], res[1])


def _sb_prompt(q_h, k_t, v_b, tq):
    s = k_t.shape[1]
    tri = jnp.asarray(np.tril(np.ones((tq, tq), np.float32), -1), BF16)
    return pl.pallas_call(
        functools.partial(_sb_kernel, tq=tq),
        grid=(SB_HEADS // 2, s // tq),
        in_specs=[pl.BlockSpec((2, tq, HEAD_DIM), lambda p, i: (p, i, 0)),
                  pl.BlockSpec((LANES, s), lambda p, i: (p, 0)),
                  pl.BlockSpec((s, LANES), lambda p, i: (0, p)),
                  pl.BlockSpec((tq, tq), lambda p, i: (0, 0))],
        out_specs=pl.BlockSpec((tq, LANES), lambda p, i: (i, p)),
        out_shape=jax.ShapeDtypeStruct((s, SB_WIDTH), F32),
        compiler_params=_cparams(("parallel", "parallel")),
        name="sb_prompt",
    )(q_h, k_t, v_b, tri)


def _compress_rows(x, pe_a, pe_b, w_a, w_b, b1, w2):
    m = x.shape[0]
    a = _dot((x + pe_a).astype(BF16), w_a)
    b = _dot((x + pe_b).astype(BF16), w_b)
    b = pltpu.roll(b, m - 1, 0)
    return jax.nn.gelu(a + b + b1)


def _compress_kernel(x_ref, pea_ref, peb_ref, wa_ref, wb_ref, b1_ref, w2_ref, w2t_ref, w2a_ref,
                     cmp_t_ref, cmp_a_ref):
    hid = _compress_rows(x_ref[...], pea_ref[...], peb_ref[...], wa_ref[...], wb_ref[...],
                         b1_ref[...], None).astype(BF16)
    cmp_t_ref[...] = _dot_nt(w2t_ref[...], hid).astype(BF16)
    for kv in range(NSA_KV):
        cmp_a_ref[kv] = _dot(hid, w2a_ref[kv]).astype(BF16)


def _compress_prompt(x2, cw):
    m = x2.shape[1]
    lead = lambda *shape: pl.BlockSpec((None,) + shape, lambda i: (i,) + (0,) * len(shape))
    return pl.pallas_call(
        _compress_kernel,
        grid=(2,),
        in_specs=[lead(m, 16 * KV_WIDTH), lead(1, 16 * KV_WIDTH), lead(1, 16 * KV_WIDTH),
                  lead(16 * KV_WIDTH, 2 * CMP_HIDDEN), lead(16 * KV_WIDTH, 2 * CMP_HIDDEN),
                  lead(1, 2 * CMP_HIDDEN), lead(2 * CMP_HIDDEN, KV_WIDTH),
                  lead(KV_WIDTH, 2 * CMP_HIDDEN), lead(NSA_KV, 2 * CMP_HIDDEN, LANES)],
        out_specs=(lead(KV_WIDTH, m), lead(NSA_KV, m, LANES)),
        out_shape=(jax.ShapeDtypeStruct((2, KV_WIDTH, m), BF16),
                   jax.ShapeDtypeStruct((2, NSA_KV, m, LANES), BF16)),
        compiler_params=_cparams(("parallel",)),
        name="compress_prompt",
    )(x2, cw["pe_a"], cw["pe_b"], cw["w_a"], cw["w_b"], cw["b1"], cw["w2"], cw["w2t"], cw["w2a"])


def _compress_weights(pe, w1, b1, w2):
    half = CMP_BLOCK // 2
    eye = jnp.eye(NSA_KV, dtype=F32)
    w1r = w1.reshape(2, half, HEAD_DIM, CMP_HIDDEN)

    def big(wh):
        return jnp.einsum("rdf,pk->rpdkf", wh, eye).reshape(half * KV_WIDTH, NSA_KV * CMP_HIDDEN)

    def perow(p):
        return jnp.broadcast_to(p[:, None, :], (half, NSA_KV, HEAD_DIM)).reshape(1, half * KV_WIDTH)

    w2bd = jnp.einsum("fd,pk->pfkd", w2, eye).reshape(NSA_KV * CMP_HIDDEN, KV_WIDTH)
    w2a = jnp.stack([jnp.concatenate(
        [w2bd[:, kv * HEAD_DIM:(kv + 1) * HEAD_DIM], jnp.zeros((NSA_KV * CMP_HIDDEN, LANES - HEAD_DIM), F32)],
        axis=1) for kv in range(NSA_KV)])
    return dict(pe_a=perow(pe[:half]), pe_b=perow(pe[half:]),
                w_a=big(w1r[0]).astype(BF16), w_b=big(w1r[1]).astype(BF16),
                b1=jnp.tile(b1, NSA_KV)[None], w2=w2bd.astype(BF16), w2t=w2bd.T.astype(BF16),
                w2a=w2a.astype(BF16))


def _top_select(vals, idx, forced, axis, rounds):
    sel = forced
    for _ in range(rounds):
        mx = jnp.max(vals, axis=axis, keepdims=True)
        first = jnp.min(jnp.where(vals == mx, idx, jnp.int32(1 << 30)), axis=axis, keepdims=True)
        hit = jnp.logical_and(idx == first, mx > 0.5 * NEG)
        sel = jnp.logical_or(sel, hit)
        vals = jnp.where(hit, NEG, vals)
    return sel


def _pack_heads(o, tq):
    lane = lax.broadcasted_iota(jnp.int32, (tq, LANES), 1)
    o4 = o.reshape(NSA_GROUP, tq, LANES)
    halves = []
    for a in range(NSA_GROUP // 2):
        halves.append(jnp.where(lane < HEAD_DIM, o4[2 * a], pltpu.roll(o4[2 * a + 1], HEAD_DIM, 1)))
    return halves


def _nsa_cmp_kernel(q_ref, ltab_ref, kt_ref, v_ref, ov_ref, o_ref, sel_ref, *, tq, n_blk):
    qi = pl.program_id(1)
    rows = NSA_GROUP * tq
    n_c = kt_ref.shape[1]
    q4 = q_ref[...].reshape(rows, HEAD_DIM)
    r = lax.broadcasted_iota(jnp.int32, (HEAD_DIM, n_c), 0)
    c = lax.broadcasted_iota(jnp.int32, (HEAD_DIM, n_c), 1)
    place = jnp.where(jnp.logical_and(r < 32, c == (tq // CMP_STRIDE) * qi + (r & 15) - 9), 1.0, 0.0)
    s = _dot(q4, kt_ref[...]) + _dot(ltab_ref[...], place.astype(BF16))
    qpos = qi * tq + (lax.broadcasted_iota(jnp.int32, (rows, 1), 0) & (tq - 1))
    c_max = (qpos - (CMP_BLOCK - 1)) >> 4
    mask = lax.broadcasted_iota(jnp.int32, (rows, n_c), 1) <= c_max
    s = jnp.where(mask, s, NEG)
    m = jnp.max(s, axis=-1, keepdims=True)
    e = jnp.where(mask, jnp.exp(s - m), 0.0)
    p = e / jnp.maximum(jnp.sum(e, axis=-1, keepdims=True), 1e-30)
    o = _dot(p.astype(BF16), v_ref[...])
    h0, h1 = _pack_heads(o, tq)
    o_ref[:, 0:LANES] = h0
    o_ref[:, LANES:2 * LANES] = h1
    p4 = p.reshape(NSA_GROUP, tq, n_c)
    pg = p4[0] + p4[1] + p4[2] + p4[3]
    imp = _dot(pg.astype(BF16), ov_ref[...])
    j = lax.broadcasted_iota(jnp.int32, (tq, n_blk), 1)
    q_blk = (qi * tq + lax.broadcasted_iota(jnp.int32, (tq, 1), 0)) >> 6
    forced = jnp.logical_or(j == 0, jnp.logical_and(j >= q_blk - 1, j <= q_blk))
    cand = jnp.logical_and(j >= 1, j <= q_blk - 2)
    sel = _top_select(jnp.where(cand, imp, NEG), j, forced, 1, N_SELECT - N_FORCED)
    sel_ref[...] = jnp.where(sel, 1.0, 0.0).astype(BF16)


def _nsa_cmp_prompt(qn_h, ltab, cmp_kt, cmp_va, overlap, tq):
    s = qn_h.shape[1]
    n_c = cmp_kt.shape[2]
    n_blk = overlap.shape[1]
    return pl.pallas_call(
        functools.partial(_nsa_cmp_kernel, tq=tq, n_blk=n_blk),
        grid=(NSA_KV, s // tq),
        in_specs=[pl.BlockSpec((NSA_GROUP, tq, HEAD_DIM), lambda k, i: (k, i, 0)),
                  pl.BlockSpec((None, NSA_GROUP * tq, HEAD_DIM), lambda k, i: (k, 0, 0)),
                  pl.BlockSpec((None, HEAD_DIM, n_c), lambda k, i: (k, 0, 0)),
                  pl.BlockSpec((None, n_c, LANES), lambda k, i: (k, 0, 0)),
                  pl.BlockSpec((n_c, n_blk), lambda k, i: (0, 0))],
        out_specs=(pl.BlockSpec((tq, 2 * LANES), lambda k, i: (i, k)),
                   pl.BlockSpec((None, tq, n_blk), lambda k, i: (k, i, 0))),
        out_shape=(jax.ShapeDtypeStruct((s, NSA_WIDTH), F32),
                   jax.ShapeDtypeStruct((NSA_KV, s, n_blk), BF16)),
        compiler_params=_cparams(("parallel", "parallel")),
        name="nsa_cmp_prompt",
    )(qn_h, ltab, cmp_kt, cmp_va, overlap)


def _nsa_sel_kernel(q_ref, kt_ref, v_ref, sel_ref, pn_ref, o_ref, *, tq, n_blk):
    qi = pl.program_id(1)
    rows = NSA_GROUP * tq
    tk_far = SEL_PAD + LANES
    q4 = q_ref[...].reshape(rows, HEAD_DIM)
    selm = sel_ref[...]

    def tile(start, tk, bias, m, acc):
        ps = pl.multiple_of(start + SEL_PAD, LANES)
        jb = lax.broadcasted_iota(jnp.int32, (n_blk, tk), 0)
        lk = lax.broadcasted_iota(jnp.int32, (n_blk, tk), 1)
        expand = jnp.where(jb == ((start + lk) >> 6), 1.0, 0.0).astype(BF16)
        drop = (_dot(selm, expand) - 1.0) * (-NEG)
        s = _dot(q4, kt_ref[:, pl.ds(ps, tk)])
        if bias is not None:
            s = s + bias
        s3 = s.reshape(NSA_GROUP, tq, tk) + drop[None]
        m_new = jnp.maximum(m, jnp.max(s3, axis=-1, keepdims=True))
        p = jnp.exp(s3 - m_new).reshape(rows, tk)
        alpha = jnp.exp(m - m_new).reshape(rows, 1)
        acc = alpha * acc + _dot(p.astype(BF16), v_ref[pl.ds(ps, tk), :])
        return m_new, acc

    near = (qi - 1) * tq
    m0 = jnp.full((NSA_GROUP, tq, 1), NEG, F32)
    m, acc = tile(near, 2 * tq, pn_ref[...], m0, jnp.zeros((rows, LANES), F32))

    def body(t, c):
        return tile(near - tk_far * (t + 1), tk_far, None, *c)

    m, acc = lax.fori_loop(0, (near + tk_far - 1) // tk_far, body, (m, acc))
    o = acc / jnp.maximum(acc[:, HEAD_DIM:HEAD_DIM + 1], 1e-30)
    h0, h1 = _pack_heads(o, tq)
    o_ref[:, 0:LANES] = h0
    o_ref[:, LANES:2 * LANES] = h1


def _nsa_sel_prompt(qn_h, ks_t, vs_a, sel, p_near, tq):
    s = qn_h.shape[1]
    sp = ks_t.shape[2]
    n_blk = sel.shape[2]
    return pl.pallas_call(
        functools.partial(_nsa_sel_kernel, tq=tq, n_blk=n_blk),
        grid=(NSA_KV, s // tq),
        in_specs=[pl.BlockSpec((NSA_GROUP, tq, HEAD_DIM), lambda k, i: (k, i, 0)),
                  pl.BlockSpec((None, HEAD_DIM, sp), lambda k, i: (k, 0, 0)),
                  pl.BlockSpec((None, sp, LANES), lambda k, i: (k, 0, 0)),
                  pl.BlockSpec((None, tq, n_blk), lambda k, i: (k, i, 0)),
                  pl.BlockSpec((None, NSA_GROUP * tq, 2 * tq), lambda k, i: (k, 0, 0))],
        out_specs=pl.BlockSpec((tq, 2 * LANES), lambda k, i: (i, k)),
        out_shape=jax.ShapeDtypeStruct((s, NSA_WIDTH), F32),
        compiler_params=_cparams(("parallel", "parallel")),
        name="nsa_sel_prompt",
    )(qn_h, ks_t, vs_a, sel, p_near)


def _nsa_win_kernel(q_ref, kt_ref, v_ref, pw_ref, o_ref, *, tq):
    qi = pl.program_id(1)
    rows = NSA_GROUP * tq
    span = WINDOW + tq
    q4 = q_ref[...].reshape(rows, HEAD_DIM)
    ps = pl.multiple_of(qi * tq, LANES)
    s = _dot(q4, kt_ref[:, pl.ds(ps, span)]) + pw_ref[...]
    real = lax.broadcasted_iota(jnp.int32, (rows, span), 1) >= WIN_PAD - qi * tq
    s = jnp.where(real, s, NEG)
    m = jnp.max(s, axis=-1, keepdims=True)
    e = jnp.where(real, jnp.exp(s - m), 0.0)
    acc = _dot(e.astype(BF16), v_ref[pl.ds(ps, span), :])
    o = acc / jnp.maximum(acc[:, HEAD_DIM:HEAD_DIM + 1], 1e-30)
    h0, h1 = _pack_heads(o, tq)
    o_ref[:, 0:LANES] = h0
    o_ref[:, LANES:2 * LANES] = h1


def _nsa_win_prompt(qn_h, kw_t, vw_a, p_win, tq):
    s = qn_h.shape[1]
    sp = kw_t.shape[2]
    return pl.pallas_call(
        functools.partial(_nsa_win_kernel, tq=tq),
        grid=(NSA_KV, s // tq),
        in_specs=[pl.BlockSpec((NSA_GROUP, tq, HEAD_DIM), lambda k, i: (k, i, 0)),
                  pl.BlockSpec((None, HEAD_DIM, sp), lambda k, i: (k, 0, 0)),
                  pl.BlockSpec((None, sp, LANES), lambda k, i: (k, 0, 0)),
                  pl.BlockSpec((None, NSA_GROUP * tq, WINDOW + tq), lambda k, i: (k, 0, 0))],
        out_specs=pl.BlockSpec((tq, 2 * LANES), lambda k, i: (i, k)),
        out_shape=jax.ShapeDtypeStruct((s, NSA_WIDTH), F32),
        compiler_params=_cparams(("parallel", "parallel")),
        name="nsa_win_prompt",
    )(qn_h, kw_t, vw_a, p_win)


def _route(logits):
    lane = lax.broadcasted_iota(jnp.int32, logits.shape, 1)
    big = jnp.int32(1 << 30)
    is_g = jnp.logical_and(lane >= N_EXPERTS, lane < N_EXPERTS + N_GROUPS)
    gl = jnp.where(is_g, logits, NEG)
    gmax = jnp.max(gl, axis=-1, keepdims=True)
    gidx = jnp.min(jnp.where(gl == gmax, lane - N_EXPERTS, big), axis=-1, keepdims=True)
    g_w = 1.0 / jnp.sum(jnp.where(is_g, jnp.exp(gl - gmax), 0.0), axis=-1, keepdims=True)
    in_g = jnp.logical_and(lane < N_EXPERTS, (lane >> 3) == gidx)
    el = jnp.where(in_g, logits, NEG)
    emax = jnp.max(el, axis=-1, keepdims=True)
    ee = jnp.where(in_g, jnp.exp(el - emax), 0.0)
    prob = jnp.where(in_g, ee / jnp.sum(ee, axis=-1, keepdims=True), -1.0)
    p1 = jnp.max(prob, axis=-1, keepdims=True)
    i1 = jnp.min(jnp.where(prob == p1, lane, big), axis=-1, keepdims=True)
    prob2 = jnp.where(lane == i1, -1.0, prob)
    p2 = jnp.max(prob2, axis=-1, keepdims=True)
    i2 = jnp.min(jnp.where(prob2 == p2, lane, big), axis=-1, keepdims=True)
    tot = p1 + p2
    return jnp.where(lane == i1, g_w * (p1 / tot), jnp.where(lane == i2, g_w * (p2 / tot), 0.0))


def _merge_kernel(x_ref, osb_ref, oc_ref, os_ref, ow_ref, gmix_ref, wg_ref, eg_ref, wbs_ref, wbn_ref,
                  wo_ref, gffn_ref, wr_ref, h_ref, hn_ref, comb_ref):
    mm = lambda a, w_ref: _dot(a.astype(BF16), w_ref[...])
    x = x_ref[...]
    u = mm(_rms(x, gmix_ref[...]), wg_ref)
    g = jax.nn.sigmoid(u[:, 0:LANES])
    g_hi = _r16(g)
    g_mid = _r16(g - g_hi)
    g_br = mm(g_hi, eg_ref) + mm(g_mid, eg_ref) + mm(g - g_hi - g_mid, eg_ref)
    o_nsa = (g_br[:, 0:NSA_WIDTH] * oc_ref[...] + g_br[:, NSA_WIDTH:2 * NSA_WIDTH] * os_ref[...]
             + g_br[:, 2 * NSA_WIDTH:3 * NSA_WIDTH] * ow_ref[...])
    merged = (jax.nn.sigmoid(u[:, LANES:LANES + D_MODEL]) * mm(osb_ref[...], wbs_ref)
              + jax.nn.sigmoid(u[:, LANES + D_MODEL:LANES + 2 * D_MODEL]) * mm(o_nsa, wbn_ref))
    h = x + mm(merged, wo_ref)
    hn = _rms(h, gffn_ref[...])
    h_ref[...] = h
    hn_ref[...] = hn.astype(BF16)
    comb_ref[...] = _route(mm(hn, wr_ref))


def _merge(x, o_sb, o_c, o_s, o_w, mw, tm):
    n = x.shape[0]
    row = lambda c: pl.BlockSpec((tm, c), lambda i: (i, 0))
    full = lambda a: pl.BlockSpec(a.shape, lambda i: (0,) * a.ndim)
    ws = (mw["g_mix"], mw["w_gate"], mw["e_g"], mw["w_b_sb"], mw["w_b_nsa"], mw["w_o"], mw["g_ffn"], mw["w_r"])
    return pl.pallas_call(
        _merge_kernel,
        grid=(n // tm,),
        in_specs=[row(D_MODEL), row(SB_WIDTH), row(NSA_WIDTH), row(NSA_WIDTH), row(NSA_WIDTH)]
                 + [full(a) for a in ws],
        out_specs=(row(D_MODEL), row(D_MODEL), row(LANES)),
        out_shape=(jax.ShapeDtypeStruct((n, D_MODEL), F32), jax.ShapeDtypeStruct((n, D_MODEL), BF16),
                   jax.ShapeDtypeStruct((n, LANES), F32)),
        compiler_params=_cparams(("parallel",)),
        name="merge",
    )(x, o_sb, o_c, o_s, o_w, *ws)


def _moe_kernel(h_ref, hn_ref, comb_ref, pe_ref, weg_ref, weu_ref, wed_ref, gple_ref, wpg_ref, wpe_ref,
                gfin_ref, y_ref, acc_ref):
    e = pl.program_id(1)

    @pl.when(e == 0)
    def _():
        acc_ref[...] = jnp.zeros_like(acc_ref)

    hn = hn_ref[...]
    lane = lax.broadcasted_iota(jnp.int32, comb_ref.shape, 1)
    ce = jnp.sum(jnp.where(lane == e, comb_ref[...], 0.0), axis=-1, keepdims=True)
    hid = jax.nn.silu(_dot(hn, weg_ref[...])) * _dot(hn, weu_ref[...]) * ce
    acc_ref[...] += _dot(hid.astype(BF16), wed_ref[...])

    @pl.when(e == N_EXPERTS - 1)
    def _():
        h = h_ref[...] + acc_ref[...]
        gate = jax.nn.sigmoid(_dot(_rms(h, gple_ref[...]).astype(BF16), wpg_ref[...]))
        h = h + gate * _dot(pe_ref[...].astype(BF16), wpe_ref[...])
        y_ref[...] = _rms(h, gfin_ref[...])


def _moe(h, hn, comb, p_emb, ew, tm):
    n = h.shape[0]
    row = lambda c: pl.BlockSpec((tm, c), lambda i, e: (i, 0))
    full = lambda a: pl.BlockSpec(a.shape, lambda i, e: (0,) * a.ndim)
    exp = lambda a, b: pl.BlockSpec((None, a, b), lambda i, e: (e, 0, 0))
    return pl.pallas_call(
        _moe_kernel,
        grid=(n // tm, N_EXPERTS),
        in_specs=[row(D_MODEL), row(D_MODEL), row(LANES), row(PLE_DIM),
                  exp(D_MODEL, EXPERT_FF), exp(D_MODEL, EXPERT_FF), exp(EXPERT_FF, D_MODEL),
                  full(ew["g_ple"]), full(ew["w_pg"]), full(ew["w_pe"]), full(ew["g_fin"])],
        out_specs=row(D_MODEL),
        out_shape=jax.ShapeDtypeStruct((n, D_MODEL), F32),
        scratch_shapes=[pltpu.VMEM((tm, D_MODEL), F32)],
        compiler_params=_cparams(("parallel", "arbitrary")),
        name="moe_ple",
    )(h, hn, comb, p_emb, ew["w_eg"], ew["w_eu"], ew["w_ed"], ew["g_ple"], ew["w_pg"], ew["w_pe"], ew["g_fin"])


def _sample1_kernel(pt_ref, qsb_ref, qp_ref, sbk_ref, sbv_ref, cck_ref, ccv_ref,
                    ssb_ref, tri_ref, snsa_ref, pea_ref, peb_ref, wa_ref, wb_ref, b1_ref, w2_ref,
                    w2a_ref, bc_ref, ovt_ref, gs_ref,
                    osb_ref, oc_ref, sel_ref,
                    x2k_ref, x2v_ref, acc_ref, run_ref, *, n_pages, n_c, n_sel):
    del pt_ref
    p = pl.program_id(1)
    page = n_pages - 1 - p

    @pl.when(p == 0)
    def _():
        acc_ref[...] = jnp.zeros_like(acc_ref)
        run_ref[...] = jnp.zeros_like(run_ref)

    z = _dot_hp(_r16(sbk_ref[...]) * _r16(qsb_ref[...]), ssb_ref[...])
    t = jnp.log(1.0 + jnp.exp(-jnp.abs(z)))
    log_keep = -(jnp.maximum(z, 0.0) + t)
    after = _dot_hp(tri_ref[...], log_keep)
    a = jnp.exp(jnp.minimum(z, 0.0) - t + after + run_ref[...])
    acc_ref[...] += _dot_tn_hp(_r16(a), _r16(sbv_ref[...]))
    run_ref[...] += jnp.sum(log_keep, axis=0, keepdims=True)

    rows = pl.ds(pl.multiple_of(page * 8, 8), 8)
    x2k_ref[rows, :] = cck_ref[...]
    x2v_ref[rows, :] = ccv_ref[...]

    @pl.when(p == n_pages - 1)
    def _():
        osb_ref[...] = acc_ref[0:8, :]
        cmp = []
        for i, x2 in enumerate((x2k_ref, x2v_ref)):
            hid = _compress_rows(x2[...], pea_ref[i], peb_ref[i], wa_ref[i], wb_ref[i], b1_ref[i], None)
            cmp.append(hid.astype(BF16))
        cmp_k = _dot(cmp[0], w2_ref[0])
        cmp_v = _dot(cmp[1], w2_ref[1])
        m_rows = cmp_k.shape[0]
        k4 = jnp.concatenate([_r16(cmp_k)] * NSA_GROUP, axis=1)
        s = _dot_hp(k4 * _r16(qp_ref[...]), snsa_ref[...]) + bc_ref[...]
        valid = lax.broadcasted_iota(jnp.int32, s.shape, 0) < n_c
        s = jnp.where(valid, s, NEG)
        m = jnp.max(s, axis=0, keepdims=True)
        e = jnp.where(valid, jnp.exp(s - m), 0.0)
        pc = e / jnp.maximum(jnp.sum(e, axis=0, keepdims=True), 1e-30)
        oc_ref[...] = _dot_tn_hp(_r16(pc), _r16(cmp_v))[0:8, :]
        imp = _dot_hp(ovt_ref[...], _r16(_dot_hp(pc, gs_ref[...])))
        j = lax.broadcasted_iota(jnp.int32, imp.shape, 0)
        forced = jnp.logical_or(j == 0, jnp.logical_and(j >= n_sel - 2, j < n_sel))
        cand = jnp.logical_and(j >= 1, j < n_sel - 2)
        sel = _top_select(jnp.where(cand, imp, NEG), j, forced, 0, N_SELECT - N_FORCED)
        sel_ref[...] = jnp.where(sel, 1.0, 0.0).reshape(sel_ref.shape)


def _sample1(pt, q_sb, q_perm, sbk, sbv, cck, ccv, consts, n_c, n_sel):
    b = q_sb.shape[0]
    n_pages = pt.shape[0] // b
    m_rows = n_pages * 8
    n_blk_pad = consts["ov_t"].shape[0]
    page_map = lambda bb, p, pt_ref: (pt_ref[bb * n_pages + n_pages - 1 - p], 0, 0)
    seq = lambda *shape: pl.BlockSpec((None,) + shape, lambda bb, p, pt_ref: (bb,) + (0,) * len(shape))
    full = lambda a: pl.BlockSpec(a.shape, lambda bb, p, pt_ref: (0,) * a.ndim)
    cs = [consts[k] for k in ("s_sb", "tri", "s_nsa", "pe_a", "pe_b", "w_a", "w_b", "b1", "w2", "w2a",
                              "b_cmp", "ov_t", "g_sum")]
    grid_spec = pltpu.PrefetchScalarGridSpec(
        num_scalar_prefetch=1,
        grid=(b, n_pages),
        in_specs=[seq(1, SB_WIDTH), seq(1, NSA_WIDTH),
                  pl.BlockSpec((None, PAGE, SB_WIDTH), page_map),
                  pl.BlockSpec((None, PAGE, SB_WIDTH), page_map),
                  pl.BlockSpec((None, 8, 16 * KV_WIDTH), page_map),
                  pl.BlockSpec((None, 8, 16 * KV_WIDTH), page_map)] + [full(a) for a in cs],
        out_specs=(seq(8, SB_WIDTH), seq(8, LANES), seq(n_blk_pad // 8, 8, LANES)),
        scratch_shapes=[pltpu.VMEM((m_rows, 16 * KV_WIDTH), F32), pltpu.VMEM((m_rows, 16 * KV_WIDTH), F32),
                        pltpu.VMEM((LANES, SB_WIDTH), F32), pltpu.VMEM((1, LANES), F32)])
    return pl.pallas_call(
        functools.partial(_sample1_kernel, n_pages=n_pages, n_c=n_c, n_sel=n_sel),
        grid_spec=grid_spec,
        out_shape=(jax.ShapeDtypeStruct((b, 8, SB_WIDTH), F32), jax.ShapeDtypeStruct((b, 8, LANES), F32),
                   jax.ShapeDtypeStruct((b, n_blk_pad // 8, 8, LANES), F32)),
        compiler_params=_cparams(("parallel", "arbitrary")),
        name="sample_pass1",
    )(pt, q_sb, q_perm, sbk, sbv, cck, ccv, *cs)


def _sample2_kernel(pt_ref, qp_ref, sk_ref, sv_ref, sel_ref, blast_ref, snsa_ref,
                    nks_ref, nvs_ref, nkw_ref, nvw_ref, wk_ref, wv_ref, bw_ref, bnew_ref,
                    os_ref, ow_ref, s_scr, v_scr, *, n_pages):
    del pt_ref
    p = pl.program_id(1)
    q_row = _r16(qp_ref[...])

    def scores(k):
        k4 = jnp.concatenate([_r16(k)] * NSA_GROUP, axis=1)
        return _dot_hp(k4 * q_row, snsa_ref[...])

    tile = sel_ref[p >> 2]
    sub = lax.broadcasted_iota(jnp.int32, tile.shape, 0)
    r0 = jnp.sum(jnp.where(sub == 2 * (p & 3), tile, 0.0), axis=0, keepdims=True)
    r1 = jnp.sum(jnp.where(sub == 2 * (p & 3) + 1, tile, 0.0), axis=0, keepdims=True)
    tok = lax.broadcasted_iota(jnp.int32, (PAGE, LANES), 0)
    keep = jnp.where(tok < SEL_BLOCK, r0, r1) > 0.5
    bias = jnp.where(p == n_pages - 1, blast_ref[...], 0.0)
    s_scr[p] = jnp.where(keep, scores(sk_ref[...]) + bias, NEG)
    v_scr[pl.ds(pl.multiple_of(p * PAGE, PAGE), PAGE), :] = sv_ref[...].astype(BF16)

    @pl.when(p == n_pages - 1)
    def _():
        first = tok == 0

        def attend(s_past, v_past, k_new_ref, v_new_ref):
            s_new = jnp.where(first, scores(jnp.where(first, k_new_ref[...], 0.0)) + bnew_ref[...], NEG)
            m = jnp.maximum(jnp.max(s_past, axis=0, keepdims=True), jnp.max(s_new, axis=0, keepdims=True))
            e_past = jnp.where(s_past > 0.5 * NEG, jnp.exp(s_past - m), 0.0)
            e_new = jnp.where(first, jnp.exp(s_new - m), 0.0)
            l = jnp.maximum(jnp.sum(e_past, axis=0, keepdims=True) + jnp.sum(e_new, axis=0, keepdims=True), 1e-30)
            o = _dot((e_past / l).T.astype(BF16), v_past)
            o = o + _r16(e_new / l).T[:, 0:1] * _r16(v_new_ref[...])
            return o[0:8, :]

        os_ref[...] = attend(s_scr[...].reshape(n_pages * PAGE, LANES), v_scr[...], nks_ref, nvs_ref)
        in_win = lax.broadcasted_iota(jnp.int32, (wk_ref.shape[0], LANES), 0) >= 1
        s_w = jnp.where(in_win, scores(wk_ref[...]) + bw_ref[...], NEG)
        ow_ref[...] = attend(s_w, wv_ref[...].astype(BF16), nkw_ref, nvw_ref)


def _sample2(pt, q_perm, sk, sv, sel, new_rows, win_k, win_v, consts):
    b = q_perm.shape[0]
    n_pages = pt.shape[0] // b
    w_rows = win_k.shape[1]
    page_map = lambda bb, p, pt_ref: (pt_ref[bb * n_pages + p], 0, 0)
    seq = lambda *shape: pl.BlockSpec((None,) + shape, lambda bb, p, pt_ref: (bb,) + (0,) * len(shape))
    full = lambda a: pl.BlockSpec(a.shape, lambda bb, p, pt_ref: (0,) * a.ndim)
    grid_spec = pltpu.PrefetchScalarGridSpec(
        num_scalar_prefetch=1,
        grid=(b, n_pages),
        in_specs=[seq(1, NSA_WIDTH),
                  pl.BlockSpec((None, PAGE, KV_WIDTH), page_map),
                  pl.BlockSpec((None, PAGE, KV_WIDTH), page_map),
                  seq(*sel.shape[1:]), full(consts["b_last"]), full(consts["s_nsa"]),
                  seq(1, KV_WIDTH), seq(1, KV_WIDTH), seq(1, KV_WIDTH), seq(1, KV_WIDTH),
                  seq(w_rows, KV_WIDTH), seq(w_rows, KV_WIDTH), full(consts["b_win"]), full(consts["b_new"])],
        out_specs=(seq(8, LANES), seq(8, LANES)),
        scratch_shapes=[pltpu.VMEM((n_pages, PAGE, LANES), F32), pltpu.VMEM((n_pages * PAGE, KV_WIDTH), BF16)])
    return pl.pallas_call(
        functools.partial(_sample2_kernel, n_pages=n_pages),
        grid_spec=grid_spec,
        out_shape=(jax.ShapeDtypeStruct((b, 8, LANES), F32), jax.ShapeDtypeStruct((b, 8, LANES), F32)),
        compiler_params=_cparams(("parallel", "arbitrary")),
        name="sample_pass2",
    )(pt, q_perm, sk, sv, sel, consts["b_last"], consts["s_nsa"], *new_rows, win_k, win_v,
      consts["b_win"], consts["b_new"])


def _sample_kernel(pt_ref, qsb_ref, qnt_ref, sbk_ref, sbv_ref, cck_ref, ccv_ref, slk_ref, slv_ref,
                   tri_ref, pe_ref, w1a_ref, w1b_ref, b1_ref, w2_ref, bc_ref, bl_ref, bw_ref, bn_ref,
                   ovt_ref, gs_ref, nks_ref, nvs_ref, nkw_ref, nvw_ref, wk_ref, wv_ref,
                   osb_ref, oc_ref, os_ref, ow_ref,
                   x2_scr, s_scr, v_scr, acc_scr, run_scr, *, n_pages, n_c, n_sel):
    del pt_ref
    p = pl.program_id(1)
    page = n_pages - 1 - p
    m_rows = n_pages * (PAGE // CMP_STRIDE)
    lane = lax.broadcasted_iota(jnp.int32, (PAGE, LANES), 1)

    @pl.when(p == 0)
    def _():
        acc_scr[...] = jnp.zeros_like(acc_scr)
        run_scr[...] = jnp.zeros_like(run_scr)

    def nsa_scores(k0, k1):
        return _dot(k0.astype(BF16), qnt_ref[0]) + _dot(k1.astype(BF16), qnt_ref[1])

    def per_kv(ref, rows):
        return [ref[pl.ds(kv, rows, stride=NSA_KV), :] for kv in range(NSA_KV)]

    run = run_scr[...]
    alive = jnp.max(jnp.where(lane[0:1] < SB_HEADS, run, NEG)) > EXP_UNDERFLOW

    @pl.when(alive)
    def _():
        qh = _r16(qsb_ref[...])
        z = jnp.zeros((PAGE, LANES), F32)
        for h in range(SB_HEADS):
            kh = _r16(sbk_ref[pl.ds(h, PAGE, stride=SB_HEADS), :])
            z = jnp.where(lane == h, jnp.sum(kh * qh[h:h + 1, :], axis=1, keepdims=True), z)
        t = jnp.log(1.0 + jnp.exp(-jnp.abs(z)))
        log_keep = -(jnp.maximum(z, 0.0) + t)
        after = _dot_hp(tri_ref[...], log_keep)
        a = _r16(jnp.exp(jnp.minimum(z, 0.0) - t + after + run))
        for h in range(SB_HEADS):
            vh = _r16(sbv_ref[pl.ds(h, PAGE, stride=SB_HEADS), :])
            acc_scr[h:h + 1, :] += jnp.sum(a[:, h:h + 1] * vh, axis=0, keepdims=True)
        run_scr[...] = run + jnp.sum(log_keep, axis=0, keepdims=True)

    crow = pl.multiple_of(page * (PAGE // CMP_STRIDE), 8)
    for i, ref in enumerate((cck_ref, ccv_ref)):
        for kv in range(NSA_KV):
            for r2 in range(CMP_STRIDE // 2):
                pair = [ref[pl.ds(NSA_KV * r + kv, PAGE // CMP_STRIDE, stride=NSA_KV * CMP_STRIDE), :]
                        for r in (2 * r2, 2 * r2 + 1)]
                x2_scr[i, pl.ds(kv * m_rows + crow, PAGE // CMP_STRIDE), r2 * LANES:(r2 + 1) * LANES] = (
                    jnp.concatenate(pair, axis=1))

    s_scr[page] = nsa_scores(*per_kv(slk_ref, PAGE))
    trow = pl.multiple_of(page * PAGE, PAGE)
    for kv, v in enumerate(per_kv(slv_ref, PAGE)):
        v_scr[kv, pl.ds(trow, PAGE), :] = v.astype(BF16)

    @pl.when(p == n_pages - 1)
    def _():
        osb_ref[...] = acc_scr[...]
        row8 = lax.broadcasted_iota(jnp.int32, (8, HEAD_DIM), 0)
        rr = lax.broadcasted_iota(jnp.int32, (LANES, LANES), 0)
        cc = lax.broadcasted_iota(jnp.int32, (LANES, LANES), 1)

        def weighted(prob, v0, v1):
            pt = prob.T.astype(BF16)
            return _dot(pt, v0.astype(BF16)), _dot(pt, v1.astype(BF16))

        def pick(o0, o1):
            return jnp.where(row8 < NSA_GROUP, o0[0:8], o1[0:8])

        cmp = []
        for i in range(2):
            x2 = x2_scr[i]
            a = _dot((x2 + pe_ref[i, 0:1, :]).astype(BF16), w1a_ref[i])
            b = _dot((x2 + pe_ref[i, 1:2, :]).astype(BF16), w1b_ref[i])
            b = pltpu.roll(b, NSA_KV * m_rows - 1, 0)
            cmp.append(_dot(jax.nn.gelu(a + b + b1_ref[i]).astype(BF16), w2_ref[i]))
        ck, cv = cmp
        s = nsa_scores(ck[0:m_rows], ck[m_rows:]) + bc_ref[...]
        valid = lax.broadcasted_iota(jnp.int32, s.shape, 0) < n_c
        s = jnp.where(valid, s, NEG)
        e = jnp.where(valid, jnp.exp(s - jnp.max(s, axis=0, keepdims=True)), 0.0)
        pc = e / jnp.maximum(jnp.sum(e, axis=0, keepdims=True), 1e-30)
        oc_ref[...] = pick(*weighted(pc, cv[0:m_rows], cv[m_rows:]))
        imp = _dot_hp(ovt_ref[...], _r16(_dot_hp(pc, gs_ref[...])))
        j = lax.broadcasted_iota(jnp.int32, imp.shape, 0)
        forced = jnp.logical_or(j == 0, jnp.logical_and(j >= n_sel - 2, j < n_sel))
        cand = jnp.logical_and(j >= 1, j < n_sel - 2)
        sel = _top_select(jnp.where(cand, imp, NEG), j, forced, 0, N_SELECT - N_FORCED)

        def attend(s_past, v0, v1, k_new_ref, v_new_ref):
            kn = [jnp.broadcast_to(k_new_ref[kv:kv + 1, :], (16, HEAD_DIM)) for kv in range(NSA_KV)]
            s_new = nsa_scores(*kn)[0:1] + bn_ref[...]
            m = jnp.maximum(jnp.max(s_past, axis=0, keepdims=True), s_new)
            e_past = jnp.where(s_past > 0.5 * NEG, jnp.exp(s_past - m), 0.0)
            e_new = jnp.exp(s_new - m)
            l = jnp.maximum(jnp.sum(e_past, axis=0, keepdims=True) + e_new, 1e-30)
            o0, o1 = weighted(e_past / l, v0, v1)
            p_new = jnp.sum(jnp.where(rr == cc, _r16(e_new / l), 0.0), axis=1, keepdims=True)
            vn = _r16(v_new_ref[...])
            return pick(o0 + p_new * vn[0:1], o1 + p_new * vn[1:2])

        pg = lax.broadcasted_iota(jnp.int32, (n_pages, PAGE, LANES), 0)
        s_all = s_scr[...] + jnp.where(pg == n_pages - 1, bl_ref[...][None], 0.0)
        blocks = 2 * n_pages
        keep = jnp.broadcast_to(jnp.where(sel, 1.0, 0.0)[0:blocks][:, None, :], (blocks, SEL_BLOCK, LANES))
        s_all = jnp.where(keep.reshape(n_pages * PAGE, LANES) > 0.5, s_all.reshape(n_pages * PAGE, LANES), NEG)
        os_ref[...] = attend(s_all, v_scr[0], v_scr[1], nks_ref, nvs_ref)
        w_rows = wk_ref.shape[0] // NSA_KV
        in_win = lax.broadcasted_iota(jnp.int32, (w_rows, LANES), 0) >= 1
        s_w = jnp.where(in_win, nsa_scores(*per_kv(wk_ref, w_rows)) + bw_ref[...], NEG)
        ow_ref[...] = attend(s_w, *per_kv(wv_ref, w_rows), nkw_ref, nvw_ref)


def _sample(pt, q_sb, qn_t, caches, new_rows, win_k, win_v, consts, n_c, n_sel):
    b = q_sb.shape[0]
    n_pages = pt.shape[0] // b
    m_rows = n_pages * (PAGE // CMP_STRIDE)
    page_map = lambda bb, p, pt_ref: (pt_ref[bb * n_pages + n_pages - 1 - p], 0, 0)
    seq = lambda *shape: pl.BlockSpec((None,) + shape, lambda bb, p, pt_ref: (bb,) + (0,) * len(shape))
    full = lambda a: pl.BlockSpec(a.shape, lambda bb, p, pt_ref: (0,) * a.ndim)
    paged = lambda a: pl.BlockSpec((None,) + a.shape[1:], page_map)
    cs = [consts[k] for k in ("tri", "pe", "w1a", "w1b", "b1", "w2", "b_cmp", "b_last", "b_win", "b_new",
                              "ov_t", "g_sum")]
    out = jax.ShapeDtypeStruct((b, 8, HEAD_DIM), F32)
    grid_spec = pltpu.PrefetchScalarGridSpec(
        num_scalar_prefetch=1,
        grid=(b, n_pages),
        in_specs=[seq(SB_HEADS, HEAD_DIM), seq(NSA_KV, HEAD_DIM, LANES)] + [paged(a) for a in caches]
                 + [full(a) for a in cs] + [seq(NSA_KV, HEAD_DIM)] * 4
                 + [seq(*win_k.shape[1:]), seq(*win_v.shape[1:])],
        out_specs=(seq(8, HEAD_DIM),) * 4,
        scratch_shapes=[pltpu.VMEM((2, NSA_KV * m_rows, CMP_STRIDE * HEAD_DIM), F32),
                        pltpu.VMEM((n_pages, PAGE, LANES), F32),
                        pltpu.VMEM((NSA_KV, n_pages * PAGE, HEAD_DIM), BF16),
                        pltpu.VMEM((SB_HEADS, HEAD_DIM), F32), pltpu.VMEM((1, LANES), F32)])
    return pl.pallas_call(
        functools.partial(_sample_kernel, n_pages=n_pages, n_c=n_c, n_sel=n_sel),
        grid_spec=grid_spec,
        out_shape=(out,) * 4,
        compiler_params=_cparams(("parallel", "arbitrary")),
        name="sample_mix",
    )(pt, q_sb, qn_t, *caches, *cs, *new_rows, win_k, win_v)


PAGES_PER_STEP = 4


def _decode_kernel(pt_ref, qcol_ref, qrow_ref, qz_ref, *refs, n_pages, n_c, n_sel):
    del pt_ref
    pps = PAGES_PER_STEP
    pages = [refs[a * pps:(a + 1) * pps] for a in range(6)]
    (tri_ref, pe_ref, w1a_ref, w1b_ref, b1_ref, w2_ref, bc_ref, bs_ref, bw_ref, bn_ref, ov_ref, ex_ref,
     nks_ref, nvs_ref, nkw_ref, nvw_ref, wk_ref, wv_ref,
     osb_ref, oc_ref, os_ref, ow_ref,
     x2_scr, stage_scr, s_scr, vt_scr, acc_scr, run_scr) = refs[6 * pps:]
    p = pl.program_id(1)
    m_rows = n_pages * (PAGE // CMP_STRIDE)
    per_page = PAGE // CMP_STRIDE
    row8 = lax.broadcasted_iota(jnp.int32, (8, HEAD_DIM), 0)
    first_kv = row8 < NSA_GROUP

    @pl.when(p == 0)
    def _():
        acc_scr[...] = jnp.zeros_like(acc_scr)
        run_scr[...] = jnp.zeros_like(run_scr)

    def nsa_scores(kt0, kt1):
        return (_dot(qz_ref[0], kt0.astype(BF16)) + _dot(qz_ref[1], kt1.astype(BF16)))[0:8]

    def pad16(a):
        return jnp.concatenate([a, jnp.zeros_like(a)], axis=0).astype(BF16)

    def by_kv(a0, a1):
        return jnp.where(first_kv, a0[0:8], a1[0:8])

    @pl.when(jnp.max(run_scr[...]) > EXP_UNDERFLOW)
    def _():
        run = run_scr[...]
        o = jnp.zeros((8, HEAD_DIM), F32)
        for u in range(pps):
            sbk_ref, sbv_ref = pages[0][u], pages[1][u]
            z = jnp.concatenate(
                [jnp.sum(_r16(sbk_ref[h]) * _r16(qcol_ref[h]), axis=0, keepdims=True) for h in range(SB_HEADS)],
                axis=0)
            t = jnp.log(1.0 + jnp.exp(-jnp.abs(z)))
            log_keep = -(jnp.maximum(z, 0.0) + t)
            after = _dot_hp(log_keep, tri_ref[...])
            a = _r16(jnp.exp(jnp.minimum(z, 0.0) - t + after + run))
            for h in range(SB_HEADS):
                oh = lax.dot_general(a, _r16(sbv_ref[h]), (((1,), (1,)), ((), ())),
                                     preferred_element_type=F32, precision=HIGHEST)
                o = o + jnp.where(row8 == h, oh, 0.0)
            run = run + jnp.sum(log_keep, axis=1, keepdims=True)
        acc_scr[...] += o
        run_scr[...] = run

    for u in range(pps):
        page = n_pages - 1 - (p * pps + u)
        cck_ref, ccv_ref, slk_ref, slv_ref = (pages[a][u] for a in range(2, 6))
        crow = pl.multiple_of(page * per_page, per_page)
        for i, ref in enumerate((cck_ref, ccv_ref)):
            for kv in range(NSA_KV):
                stage = stage_scr.at[(u * 2 + i) * NSA_KV + kv]
                stage[...] = ref[kv].T
                for r2 in range(CMP_STRIDE // 2):
                    pair = [stage[pl.ds(r, per_page, stride=CMP_STRIDE), :] for r in (2 * r2, 2 * r2 + 1)]
                    x2_scr[i, pl.ds(kv * m_rows + crow, per_page), r2 * LANES:(r2 + 1) * LANES] = (
                        jnp.concatenate(pair, axis=1))

        col = pl.ds(pl.multiple_of(page * PAGE, PAGE), PAGE)
        s_scr[:, col] = nsa_scores(slk_ref[0], slk_ref[1])
        for kv in range(NSA_KV):
            vt_scr[kv, :, col] = slv_ref[kv].astype(BF16)

    @pl.when(p == n_pages // pps - 1)
    def _():
        osb_ref[...] = acc_scr[...]
        cmp = []
        for i in range(2):
            x2 = x2_scr[i]
            a = _dot((x2 + pe_ref[i, 0:1, :]).astype(BF16), w1a_ref[i])
            b = _dot((x2 + pe_ref[i, 1:2, :]).astype(BF16), w1b_ref[i])
            b = pltpu.roll(b, NSA_KV * m_rows - 1, 0)
            cmp.append(_dot(jax.nn.gelu(a + b + b1_ref[i]).astype(BF16), w2_ref[i]))
        ck, cv = cmp
        s = (_dot_nt(qz_ref[0], ck[0:m_rows].astype(BF16))
             + _dot_nt(qz_ref[1], ck[m_rows:].astype(BF16)))[0:8] + bc_ref[...]
        valid = lax.broadcasted_iota(jnp.int32, s.shape, 1) < n_c
        s = jnp.where(valid, s, NEG)
        e = jnp.where(valid, jnp.exp(s - jnp.max(s, axis=1, keepdims=True)), 0.0)
        pc = e / jnp.maximum(jnp.sum(e, axis=1, keepdims=True), 1e-30)
        pc16 = pad16(pc)
        oc_ref[...] = by_kv(_dot(pc16, cv[0:m_rows].astype(BF16)), _dot(pc16, cv[m_rows:].astype(BF16)))
        g_first = lax.broadcasted_iota(jnp.int32, pc.shape, 0) < NSA_GROUP
        pg = jnp.where(g_first, jnp.sum(jnp.where(g_first, pc, 0.0), axis=0, keepdims=True),
                       jnp.sum(jnp.where(g_first, 0.0, pc), axis=0, keepdims=True))
        imp = _dot_hp(_r16(pg), ov_ref[...])
        j = lax.broadcasted_iota(jnp.int32, imp.shape, 1)
        forced = jnp.logical_or(j == 0, jnp.logical_and(j >= n_sel - 2, j < n_sel))
        cand = jnp.logical_and(j >= 1, j < n_sel - 2)
        sel = _top_select(jnp.where(cand, imp, NEG), j, forced, 1, N_SELECT - N_FORCED)

        def attend(s_past, vt0, vt1, k_new_ref, v_new_ref):
            kn, vn = _r16(k_new_ref[...]), _r16(v_new_ref[...])
            s_new = jnp.sum(_r16(qrow_ref[...]) * jnp.where(first_kv, kn[0:1], kn[1:2]), axis=1, keepdims=True)
            s_new = s_new + bn_ref[...]
            m = jnp.maximum(jnp.max(s_past, axis=1, keepdims=True), s_new)
            e_past = jnp.where(s_past > 0.5 * NEG, jnp.exp(s_past - m), 0.0)
            e_new = jnp.exp(s_new - m)
            l = jnp.maximum(jnp.sum(e_past, axis=1, keepdims=True) + e_new, 1e-30)
            p16 = pad16(e_past / l)
            o = by_kv(_dot_nt(p16, vt0), _dot_nt(p16, vt1))
            return o + _r16(e_new / l) * jnp.where(first_kv, vn[0:1], vn[1:2])

        keep = _dot(pad16(jnp.where(sel, 1.0, 0.0)), ex_ref[...])[0:8] > 0.5
        s_all = jnp.where(keep, s_scr[...] + bs_ref[...], NEG)
        os_ref[...] = attend(s_all, vt_scr[0], vt_scr[1], nks_ref, nvs_ref)
        in_win = lax.broadcasted_iota(jnp.int32, (8, wk_ref.shape[2]), 1) >= 1
        s_w = jnp.where(in_win, nsa_scores(wk_ref[0], wk_ref[1]) + bw_ref[...], NEG)
        ow_ref[...] = attend(s_w, wv_ref[0].astype(BF16), wv_ref[1].astype(BF16), nkw_ref, nvw_ref)


def _decode(pt, q_col, q_row, q_z, caches, new_rows, win_k, win_v, consts, n_c, n_sel):
    b = q_row.shape[0]
    n_pages = pt.shape[0] // b
    pps = PAGES_PER_STEP
    assert n_pages % pps == 0
    m_rows = n_pages * (PAGE // CMP_STRIDE)

    def page_map(u):
        return lambda bb, p, pt_ref: (pt_ref[bb * n_pages + n_pages - 1 - (p * pps + u)], 0, 0, 0)

    seq = lambda a: pl.BlockSpec((None,) + a.shape[1:], lambda bb, p, pt_ref: (bb,) + (0,) * (a.ndim - 1))
    full = lambda a: pl.BlockSpec(a.shape, lambda bb, p, pt_ref: (0,) * a.ndim)
    paged = [pl.BlockSpec((None,) + a.shape[1:], page_map(u)) for a in caches for u in range(pps)]
    paged_args = [a for a in caches for _ in range(pps)]
    cs = [consts[k] for k in ("tri", "pe", "w1a", "w1b", "b1", "w2", "b_cmp", "b_sel", "b_win", "b_new",
                              "ov", "expand")]
    per_seq = [*new_rows, win_k, win_v]
    out = jax.ShapeDtypeStruct((b, 8, HEAD_DIM), F32)
    grid_spec = pltpu.PrefetchScalarGridSpec(
        num_scalar_prefetch=1,
        grid=(b, n_pages // pps),
        in_specs=[seq(q_col), seq(q_row), seq(q_z)] + paged + [full(a) for a in cs] + [seq(a) for a in per_seq],
        out_specs=(pl.BlockSpec((None, 8, HEAD_DIM), lambda bb, p, pt_ref: (bb, 0, 0)),) * 4,
        scratch_shapes=[pltpu.VMEM((2, NSA_KV * m_rows, CMP_STRIDE * HEAD_DIM), F32),
                        pltpu.VMEM((pps * 2 * NSA_KV, PAGE, HEAD_DIM), F32),
                        pltpu.VMEM((8, n_pages * PAGE), F32),
                        pltpu.VMEM((NSA_KV, HEAD_DIM, n_pages * PAGE), BF16),
                        pltpu.VMEM((8, HEAD_DIM), F32), pltpu.VMEM((8, 1), F32)])
    return pl.pallas_call(
        functools.partial(_decode_kernel, n_pages=n_pages, n_c=n_c, n_sel=n_sel),
        grid_spec=grid_spec,
        out_shape=(out,) * 4,
        compiler_params=_cparams(("parallel", "arbitrary")),
        name="decode_mix",
    )(pt, q_col, q_row, q_z, *paged_args, *cs, *per_seq)


def _onehot_rows(delta, dist):
    oh = (np.arange(MAX_DISTANCE)[:, None] == np.asarray(dist)[None, :]).astype(np.float32)
    return jnp.dot(delta, jnp.asarray(oh), precision=HIGHEST)


def _bias_delta(table):
    bucket = _t5_bucket_np(np.arange(MAX_DISTANCE))
    oh = (np.arange(N_BUCKETS)[:, None] == bucket[None, :]).astype(np.float32)
    return jnp.dot((table - table[N_BUCKETS - 1][None]).T, jnp.asarray(oh), precision=HIGHEST)


def _toeplitz(f, n, w):
    lf = n + w - 1
    x = jnp.broadcast_to(f[:, None, :], (f.shape[0], n, lf))
    x = jnp.pad(x, ((0, 0), (0, 0), (0, 1))).reshape(f.shape[0], n * (lf + 1))[:, :n * lf]
    return x.reshape(f.shape[0], n, lf)[:, :, n - 1:n - 1 + w]


def _pad_lanes(a, n=LANES):
    return jnp.concatenate([a, jnp.zeros(a.shape[:-1] + (n - a.shape[-1],), a.dtype)], axis=-1)


def kernel(x_prompt, x_sample, cache_sb_k, cache_sb_v, cache_cmp_k, cache_cmp_v, cache_sel_k, cache_sel_v, state_win_k, state_win_v, page_table, p_prompt, p_sample, norm_mix, w_in, cmp_pe_k, cmp_w1_k, cmp_b1_k, cmp_w2_k, cmp_pe_v, cmp_w1_v, cmp_b1_v, cmp_w2_v, rel_bias_table, w_branch_sb, w_branch_nsa, w_out, norm_ffn, w_router_group, w_router_expert, w_exp_gate, w_exp_up, w_exp_down, norm_ple, w_ple_gate, w_ple, norm_final):
    assert w_in.shape[0] == 1 and x_prompt.shape[0] == 1 and x_sample.shape[1] == 1
    seq = x_prompt.shape[1]
    n_seq = x_sample.shape[0]
    n_pool = cache_sb_k.shape[1]
    n_pages = page_table.shape[1]
    past = n_pages * PAGE
    w_buf = state_win_k.shape[2]
    tq = 128
    assert seq % 512 == 0 and w_buf == WINDOW and past >= WINDOW and seq >= WINDOW

    scale = HEAD_DIM ** -0.5
    w = w_in[0]
    o_g = 2048 + 6 * KV_WIDTH
    w_q_sb, w_k_sb, w_v_sb, w_q_n = (w[:, i * 512:(i + 1) * 512] for i in range(4))
    w_kv6 = w[:, 2048:o_g]
    w_vs, w_vw = w[:, 2048 + 3 * KV_WIDTH:2048 + 4 * KV_WIDTH], w[:, 2048 + 5 * KV_WIDTH:o_g]
    w_ks, w_kw = w[:, 2048 + 2 * KV_WIDTH:2048 + 3 * KV_WIDTH], w[:, 2048 + 4 * KV_WIDTH:2048 + 5 * KV_WIDTH]
    aug = lambda wv, kv: _pad_lanes(wv[:, kv * HEAD_DIM:(kv + 1) * HEAD_DIM])
    w_all = jnp.concatenate([w_q_sb * scale, w_k_sb, w_v_sb, w_q_n * scale, w_kv6,
                             aug(w_vs, 0), aug(w_vs, 1), aug(w_vw, 0), aug(w_vw, 1)], axis=1).astype(BF16)
    w_t = jnp.concatenate([w_k_sb.T, w_ks.T, w_kw.T], axis=0).astype(BF16)
    g_mix = norm_mix[0][None]
    w_gate_f = jnp.concatenate([_pad_lanes(w[:, o_g:o_g + 3 * NSA_HEADS]), w[:, o_g + 3 * NSA_HEADS:]], axis=1)
    e_g = np.zeros((LANES, 3 * NSA_WIDTH), np.float32)
    for h in range(NSA_HEADS):
        for br in range(3):
            e_g[h * 3 + br, br * NSA_WIDTH + h * HEAD_DIM:br * NSA_WIDTH + (h + 1) * HEAD_DIM] = 1.0
    w_r = jnp.concatenate([w_router_expert[0], w_router_group[0],
                           jnp.zeros((D_MODEL, LANES - N_EXPERTS - N_GROUPS), F32)], axis=1)
    mw = dict(g_mix=g_mix, w_gate=w_gate_f.astype(BF16), e_g=jnp.asarray(e_g, BF16),
              w_b_sb=w_branch_sb[0].astype(BF16), w_b_nsa=w_branch_nsa[0].astype(BF16),
              w_o=w_out[0].astype(BF16), g_ffn=norm_ffn[0][None], w_r=w_r.astype(BF16))
    ew = dict(w_eg=w_exp_gate[0].astype(BF16), w_eu=w_exp_up[0].astype(BF16), w_ed=w_exp_down[0].astype(BF16),
              g_ple=norm_ple[0][None], w_pg=w_ple_gate[0].astype(BF16), w_pe=w_ple[0].astype(BF16),
              g_fin=norm_final[None])
    cw_k = _compress_weights(cmp_pe_k[0], cmp_w1_k[0], cmp_b1_k[0], cmp_w2_k[0])
    cw_v = _compress_weights(cmp_pe_v[0], cmp_w1_v[0], cmp_b1_v[0], cmp_w2_v[0])
    cw = {k: jnp.stack([cw_k[k], cw_v[k]]) for k in cw_k}

    delta = _bias_delta(rel_bias_table)
    rows = NSA_GROUP * tq
    by_kv = lambda a: a.reshape((NSA_KV, rows) + a.shape[2:])
    d_c = (np.arange(tq)[:, None] - CMP_STRIDE * (np.arange(16)[None] - 9) - (CMP_BLOCK - 1)).reshape(-1)
    l_tab = by_kv(_onehot_rows(delta, d_c).reshape(NSA_HEADS, tq, 16))
    l_hi = l_tab.astype(BF16)
    l_lo = (l_tab - l_hi.astype(F32)).astype(BF16)
    l_tab = jnp.concatenate([l_hi, l_lo, jnp.zeros((NSA_KV, rows, HEAD_DIM - 32), BF16)], axis=-1)
    d_n = 2 * tq - 1 - np.arange(3 * tq - 1)
    f_n = _onehot_rows(delta, d_n) + jnp.asarray(np.where(d_n < 0, NEG, 0.0), F32)[None]
    p_near = by_kv(_toeplitz(f_n, tq, 2 * tq))
    d_w = WINDOW + tq - 1 - np.arange(WINDOW + 2 * tq - 1)
    f_w = _onehot_rows(delta, d_w) + jnp.asarray(np.where((d_w < 0) | (d_w >= WINDOW), NEG, 0.0), F32)[None]
    p_win = by_kv(_toeplitz(f_w, tq, WINDOW + tq))

    xp = x_prompt[0]
    (q_sb_h, k_sb, v_sb, v_sb_b, q_n_h, k_c, v_c, k_s, v_s, k_w, v_w, vs_a, vw_a,
     k_sb_t, k_s_t, k_w_t) = _proj(xp, g_mix, w_all, w_t, 512)
    o_sb = _sb_prompt(q_sb_h, k_sb_t, v_sb_b, 256)

    x2 = jnp.stack([k_c, v_c]).reshape(2, seq // CMP_STRIDE, CMP_STRIDE * KV_WIDTH)
    cmp_t, cmp_a = _compress_prompt(x2, cw)
    n_cp = seq // CMP_STRIDE
    n_blk = seq // SEL_BLOCK
    c_start = np.arange(n_cp)[:, None] * CMP_STRIDE
    b_start = np.arange(n_blk)[None] * SEL_BLOCK
    overlap = ((c_start < b_start + SEL_BLOCK) & (c_start + CMP_BLOCK > b_start)
               & (np.arange(n_cp)[:, None] < n_cp - 1))
    overlap = jnp.asarray(overlap.astype(np.float32), BF16)
    o_c, sel = _nsa_cmp_prompt(q_n_h, l_tab, cmp_t[0].reshape(NSA_KV, HEAD_DIM, n_cp), cmp_a[1], overlap, tq)

    front = lambda a, n, axis: jnp.pad(a, [(n, 0) if ax == axis else (0, 0) for ax in range(a.ndim)])
    o_s = _nsa_sel_prompt(q_n_h, front(k_s_t.reshape(NSA_KV, HEAD_DIM, seq), SEL_PAD, 2),
                          front(vs_a, SEL_PAD, 1), sel, p_near, tq)
    o_w = _nsa_win_prompt(q_n_h, front(k_w_t.reshape(NSA_KV, HEAD_DIM, seq), WIN_PAD, 2),
                          front(vw_a, WIN_PAD, 1), p_win, tq)
    h_p, hn_p, comb_p = _merge(xp, o_sb, o_c, o_s, o_w, mw, 512)
    y_prompt = _moe(h_p, hn_p, comb_p, p_prompt[0, 0], ew, min(1024, seq))[None]

    w_s = jnp.concatenate([w_q_sb * scale, w_k_sb, w_v_sb, w_q_n * scale, w_kv6], axis=1).astype(BF16)
    xs = x_sample[:, 0]
    u = _proj_s(xs, g_mix, w_s)
    q_sb_s, k_sb_s, v_sb_s, q_n_s = (u[:, i * 512:(i + 1) * 512] for i in range(4))
    kc_s, vc_s, ks_s, vs_s, kw_s, vw_s = (u[:, 2048 + i * KV_WIDTH:2048 + (i + 1) * KV_WIDTH] for i in range(6))
    q_g = q_n_s.reshape(n_seq, NSA_KV, NSA_GROUP, HEAD_DIM)
    q_z = jnp.stack([jnp.pad(q_g[:, kv], ((0, 0), (kv * NSA_GROUP, 16 - (kv + 1) * NSA_GROUP), (0, 0)))
                     for kv in range(NSA_KV)], axis=1).astype(BF16)

    n_c = (past + 1 - CMP_BLOCK) // CMP_STRIDE + 1
    n_sel = -(-(past + 1) // SEL_BLOCK)
    m_rows = past // CMP_STRIDE
    n_blk_pad = -(-n_sel // LANES) * LANES
    tri_s = np.tril(np.ones((PAGE, PAGE), np.float32), -1)
    ov = np.zeros((m_rows, n_blk_pad), np.float32)
    cs = np.arange(n_c) * CMP_STRIDE
    for jb in range(n_sel):
        ov[:n_c, jb] = (cs < (jb + 1) * SEL_BLOCK) & (cs + CMP_BLOCK > jb * SEL_BLOCK)
    expand = jnp.arange(n_blk_pad)[:, None] == jnp.arange(past)[None, :] // SEL_BLOCK
    b_cmp = _onehot_rows(delta, past - (np.arange(m_rows) * CMP_STRIDE + CMP_BLOCK - 1))
    b_sel = jnp.concatenate([jnp.zeros((NSA_HEADS, past - PAGE), F32),
                             _onehot_rows(delta, PAGE - np.arange(PAGE))], axis=1)
    b_win = _onehot_rows(delta, w_buf - np.arange(w_buf))
    b_new = _onehot_rows(delta, np.zeros((1,), np.int64))
    half = CMP_BLOCK // 2
    cmp_p = ((cmp_pe_k[0], cmp_w1_k[0], cmp_b1_k[0], cmp_w2_k[0]), (cmp_pe_v[0], cmp_w1_v[0], cmp_b1_v[0], cmp_w2_v[0]))
    stack = lambda f, dt=F32: jnp.stack([f(*c) for c in cmp_p]).astype(dt)
    consts = dict(tri=jnp.asarray(tri_s),
                  pe=stack(lambda pe, w1, b1, w2: pe.reshape(2, half * HEAD_DIM)),
                  w1a=stack(lambda pe, w1, b1, w2: w1[:half * HEAD_DIM], BF16),
                  w1b=stack(lambda pe, w1, b1, w2: w1[half * HEAD_DIM:], BF16),
                  b1=stack(lambda pe, w1, b1, w2: b1[None]), w2=stack(lambda pe, w1, b1, w2: w2, BF16),
                  b_cmp=b_cmp, b_sel=b_sel, b_win=b_win, b_new=b_new,
                  ov=jnp.asarray(ov), expand=jnp.asarray(expand, BF16))

    native = lambda c: jnp.transpose(c[0], (0, 2, 3, 1))
    caches = [native(c) for c in (cache_sb_k, cache_sb_v, cache_cmp_k, cache_cmp_v, cache_sel_k, cache_sel_v)]
    new_rows = [a.reshape(n_seq, NSA_KV, HEAD_DIM) for a in (ks_s, vs_s, kw_s, vw_s)]
    o4 = _decode(page_table.reshape(-1), q_sb_s.reshape(n_seq, SB_HEADS, HEAD_DIM, 1),
                 q_n_s.reshape(n_seq, NSA_HEADS, HEAD_DIM), q_z, caches, new_rows,
                 native(state_win_k), native(state_win_v), consts, n_c, n_sel)
    o_sb_s, o_c_s, o_s_s, o_w_s = (o.reshape(n_seq, SB_WIDTH) for o in o4)
    h_s, hn_s, comb_s = _merge(xs, o_sb_s, o_c_s, o_s_s, o_w_s, mw, n_seq)
    y_sample = _moe(h_s, hn_s, comb_s, p_sample[0, :, 0], ew, n_seq)[:, None]

    hd = lambda a, n: a.reshape((1, a.shape[0]) + (() if a.ndim == 2 else ()) + (n, HEAD_DIM))
    pr = lambda a, n: a.reshape(1, 1, seq, n, HEAD_DIM)
    sm = lambda a, n: a.reshape(1, n_seq, 1, n, HEAD_DIM)
    win_p = lambda a: a[seq - WINDOW:].reshape(1, 1, WINDOW, NSA_KV, HEAD_DIM)
    win_s = lambda st, new: jnp.concatenate([st[0], new.reshape(n_seq, 1, NSA_KV, HEAD_DIM)], axis=1)[None, :, 1:]
    return (y_prompt, y_sample,
            pr(k_sb, SB_HEADS), pr(v_sb, SB_HEADS), pr(k_c, NSA_KV), pr(v_c, NSA_KV), pr(k_s, NSA_KV),
            pr(v_s, NSA_KV), win_p(k_w), win_p(v_w),
            sm(k_sb_s, SB_HEADS), sm(v_sb_s, SB_HEADS), sm(kc_s, NSA_KV), sm(vc_s, NSA_KV), sm(ks_s, NSA_KV),
            sm(vs_s, NSA_KV), win_s(state_win_k, kw_s), win_s(state_win_v, vw_s))
```

```python
import functools
import math

import numpy as np
import jax
import jax.numpy as jnp
from jax import lax
from jax.experimental import pallas as pl
from jax.experimental.pallas import tpu as pltpu

F32 = jnp.float32
BF16 = jnp.bfloat16
HIGHEST = lax.Precision.HIGHEST

D_MODEL = 1024
HEAD_DIM = 64
SB_HEADS = 8
NSA_HEADS = 8
NSA_KV = 2
NSA_GROUP = NSA_HEADS // NSA_KV
SB_WIDTH = SB_HEADS * HEAD_DIM
NSA_WIDTH = NSA_HEADS * HEAD_DIM
KV_WIDTH = NSA_KV * HEAD_DIM
CMP_BLOCK = 32
CMP_STRIDE = 16
CMP_HIDDEN = 256
SEL_BLOCK = 64
N_SELECT = 16
N_FORCED = 3
WINDOW = 512
N_BUCKETS = 32
MAX_DISTANCE = 128
N_GROUPS = 4
EXPERTS_PER_GROUP = 8
N_EXPERTS = N_GROUPS * EXPERTS_PER_GROUP
EXPERT_FF = 256
PLE_DIM = 256
PAGE = 128
RMS_EPS = 1e-6
NEG = -1e30
LANES = 128
EXP_UNDERFLOW = -104.0
VMEM_LIMIT = 56 * 1024 * 1024

SEL_PAD = 7 * LANES
WIN_PAD = WINDOW


def _cparams(sem):
    return pltpu.CompilerParams(dimension_semantics=sem, vmem_limit_bytes=VMEM_LIMIT)


def _rms(x, g):
    return x * lax.rsqrt(jnp.mean(x * x, axis=-1, keepdims=True) + RMS_EPS) * g


def _dot(a, b):
    return jnp.dot(a, b, preferred_element_type=F32)


def _dot_hp(a, b):
    return jnp.dot(a, b, preferred_element_type=F32, precision=HIGHEST)


def _dot_nt(a, b):
    return lax.dot_general(a, b, (((1,), (1,)), ((), ())), preferred_element_type=F32)


def _dot_tn_hp(a, b):
    return lax.dot_general(a, b, (((0,), (0,)), ((), ())), preferred_element_type=F32,
                           precision=HIGHEST)


def _r16(a):
    return a.astype(BF16).astype(F32)


def _t5_bucket_np(n):
    n = np.maximum(n, 0)
    max_exact = N_BUCKETS // 2
    nf = np.maximum(n, 1).astype(np.float32)
    large = max_exact + (np.log(nf / np.float32(max_exact)) / np.float32(math.log(MAX_DISTANCE / max_exact))
                         * np.float32(N_BUCKETS - max_exact)).astype(np.int32)
    large = np.minimum(large, N_BUCKETS - 1)
    return np.where(n < max_exact, n, large)


def _proj_kernel(x_ref, g_ref, w_ref, wt_ref,
                 qsb_ref, ksb_ref, vsb_ref, vsbb_ref, qn_ref,
                 kc_ref, vc_ref, ks_ref, vs_ref, kw_ref, vw_ref,
                 vsa_ref, vwa_ref, ksbt_ref, kst_ref, kwt_ref):
    xb = _rms(x_ref[...], g_ref[...]).astype(BF16)

    def cols(a, b):
        return _dot(xb, w_ref[:, a:b])

    u = cols(0, 512)
    for h in range(SB_HEADS):
        qsb_ref[h] = u[:, h * HEAD_DIM:(h + 1) * HEAD_DIM].astype(BF16)
    ksb_ref[...] = cols(512, 1024)
    v = cols(1024, 1536)
    vsb_ref[...] = v
    vsbb_ref[...] = v.astype(BF16)
    u = cols(1536, 2048)
    for h in range(NSA_HEADS):
        qn_ref[h] = u[:, h * HEAD_DIM:(h + 1) * HEAD_DIM].astype(BF16)
    for i, r in enumerate((kc_ref, vc_ref, ks_ref, vs_ref, kw_ref, vw_ref)):
        r[...] = cols(2048 + i * KV_WIDTH, 2048 + (i + 1) * KV_WIDTH)
    ones_col = lax.broadcasted_iota(jnp.int32, (xb.shape[0], LANES), 1) == HEAD_DIM
    base = 2048 + 6 * KV_WIDTH
    for kv in range(NSA_KV):
        a = cols(base + kv * LANES, base + (kv + 1) * LANES)
        vsa_ref[kv] = jnp.where(ones_col, 1.0, a).astype(BF16)
        a = cols(base + (NSA_KV + kv) * LANES, base + (NSA_KV + kv + 1) * LANES)
        vwa_ref[kv] = jnp.where(ones_col, 1.0, a).astype(BF16)
    ksbt_ref[...] = _dot_nt(wt_ref[0:512, :], xb).astype(BF16)
    kst_ref[...] = _dot_nt(wt_ref[512:640, :], xb).astype(BF16)
    kwt_ref[...] = _dot_nt(wt_ref[640:768, :], xb).astype(BF16)


def _proj(x, g, w_all, w_t, tm):
    s = x.shape[0]
    nw = w_all.shape[1]
    row = lambda n: pl.BlockSpec((tm, n), lambda i: (i, 0))
    head = pl.BlockSpec((SB_HEADS, tm, HEAD_DIM), lambda i: (0, i, 0))
    aug = pl.BlockSpec((NSA_KV, tm, LANES), lambda i: (0, i, 0))
    tr = lambda n: pl.BlockSpec((n, tm), lambda i: (0, i))
    sd = jax.ShapeDtypeStruct
    out_shape = (
        sd((SB_HEADS, s, HEAD_DIM), BF16), sd((s, SB_WIDTH), F32), sd((s, SB_WIDTH), F32),
        sd((s, SB_WIDTH), BF16), sd((NSA_HEADS, s, HEAD_DIM), BF16),
        *(sd((s, KV_WIDTH), F32) for _ in range(6)),
        sd((NSA_KV, s, LANES), BF16), sd((NSA_KV, s, LANES), BF16),
        sd((SB_WIDTH, s), BF16), sd((KV_WIDTH, s), BF16), sd((KV_WIDTH, s), BF16))
    out_specs = (head, row(SB_WIDTH), row(SB_WIDTH), row(SB_WIDTH), head,
                 *(row(KV_WIDTH) for _ in range(6)), aug, aug,
                 tr(SB_WIDTH), tr(KV_WIDTH), tr(KV_WIDTH))
    return pl.pallas_call(
        _proj_kernel,
        grid=(s // tm,),
        in_specs=[row(D_MODEL), pl.BlockSpec((1, D_MODEL), lambda i: (0, 0)),
                  pl.BlockSpec((D_MODEL, nw), lambda i: (0, 0)),
                  pl.BlockSpec((w_t.shape[0], D_MODEL), lambda i: (0, 0))],
        out_specs=out_specs,
        out_shape=out_shape,
        compiler_params=_cparams(("parallel",)),
        name="proj_prompt",
    )(x, g, w_all, w_t)


def _proj_s_kernel(x_ref, g_ref, w_ref, o_ref):
    o_ref[...] = _dot(_rms(x_ref[...], g_ref[...]).astype(BF16), w_ref[...])


def _proj_s(x, g, w):
    b, n = x.shape[0], w.shape[1]
    return pl.pallas_call(
        _proj_s_kernel,
        out_shape=jax.ShapeDtypeStruct((b, n), F32),
        compiler_params=pltpu.CompilerParams(vmem_limit_bytes=VMEM_LIMIT),
        name="proj_sample",
    )(x, g, w)


def _sb_kernel(q_ref, kt_ref, v_ref, tri_ref, o_ref, *, tq):
    qi = pl.program_id(1)
    tri = tri_ref[...]
    row = lax.broadcasted_iota(jnp.int32, (tq, tq), 0)
    col = lax.broadcasted_iota(jnp.int32, (tq, tq), 1)
    below = col < row
    res = []
    for hh in range(2):
        q = q_ref[hh]

        def tile(j, run, diagonal, hh=hh, q=q):
            ks = pl.multiple_of(j * tq, tq)
            kt = kt_ref[hh * HEAD_DIM:(hh + 1) * HEAD_DIM, pl.ds(ks, tq)]
            z = _dot(q, kt)
            t = jnp.log(1.0 + jnp.exp(-jnp.abs(z)))
            log_keep = -(jnp.maximum(z, 0.0) + t)
            if diagonal:
                log_keep = jnp.where(below, log_keep, 0.0)
            after = _dot(log_keep.astype(BF16), tri)
            a = jnp.exp(jnp.minimum(z, 0.0) - t + after + run)
            if diagonal:
                a = jnp.where(below, a, 0.0)
            pv = _dot(a.astype(BF16), v_ref[pl.ds(ks, tq), :])
            return pv, run + after[:, 0:1] + log_keep[:, 0:1]

        pv0, run0 = tile(qi, jnp.zeros((tq, 1), F32), True)

        def cond(c):
            return jnp.logical_and(c[0] <= qi, c[3])

        def body(c):
            jj, acc, run, _ = c
            pv, run = tile(qi - jj, run, False)
            return jj + 1, acc + pv, run, jnp.max(run) > EXP_UNDERFLOW

        c = lax.while_loop(cond, body, (jnp.int32(1), pv0, run0, jnp.max(run0) > EXP_UNDERFLOW))
        res.append(c[1])
    lane = lax.broadcasted_iota(jnp.int32, (tq, LANES), 1)
    o_ref[...] = jnp.where(lane < HEAD_DIM, res[0], res[1])


def _sb_prompt(q_h, k_t, v_b, tq):
    s = k_t.shape[1]
    tri = jnp.asarray(np.tril(np.ones((tq, tq), np.float32), -1), BF16)
    return pl.pallas_call(
        functools.partial(_sb_kernel, tq=tq),
        grid=(SB_HEADS // 2, s // tq),
        in_specs=[pl.BlockSpec((2, tq, HEAD_DIM), lambda p, i: (p, i, 0)),
                  pl.BlockSpec((LANES, s), lambda p, i: (p, 0)),
                  pl.BlockSpec((s, LANES), lambda p, i: (0, p)),
                  pl.BlockSpec((tq, tq), lambda p, i: (0, 0))],
        out_specs=pl.BlockSpec((tq, LANES), lambda p, i: (i, p)),
        out_shape=jax.ShapeDtypeStruct((s, SB_WIDTH), F32),
        compiler_params=_cparams(("parallel", "parallel")),
        name="sb_prompt",
    )(q_h, k_t, v_b, tri)


def _compress_rows(x, pe_a, pe_b, w_a, w_b, b1, w2):
    m = x.shape[0]
    a = _dot((x + pe_a).astype(BF16), w_a)
    b = _dot((x + pe_b).astype(BF16), w_b)
    b = pltpu.roll(b, m - 1, 0)
    return jax.nn.gelu(a + b + b1)


def _compress_kernel(x_ref, pea_ref, peb_ref, wa_ref, wb_ref, b1_ref, w2_ref, w2t_ref, w2a_ref,
                     cmp_t_ref, cmp_a_ref):
    hid = _compress_rows(x_ref[...], pea_ref[...], peb_ref[...], wa_ref[...], wb_ref[...],
                         b1_ref[...], None).astype(BF16)
    cmp_t_ref[...] = _dot_nt(w2t_ref[...], hid).astype(BF16)
    for kv in range(NSA_KV):
        cmp_a_ref[kv] = _dot(hid, w2a_ref[kv]).astype(BF16)


def _compress_prompt(x2, cw):
    m = x2.shape[1]
    lead = lambda *shape: pl.BlockSpec((None,) + shape, lambda i: (i,) + (0,) * len(shape))
    return pl.pallas_call(
        _compress_kernel,
        grid=(2,),
        in_specs=[lead(m, 16 * KV_WIDTH), lead(1, 16 * KV_WIDTH), lead(1, 16 * KV_WIDTH),
                  lead(16 * KV_WIDTH, 2 * CMP_HIDDEN), lead(16 * KV_WIDTH, 2 * CMP_HIDDEN),
                  lead(1, 2 * CMP_HIDDEN), lead(2 * CMP_HIDDEN, KV_WIDTH),
                  lead(KV_WIDTH, 2 * CMP_HIDDEN), lead(NSA_KV, 2 * CMP_HIDDEN, LANES)],
        out_specs=(lead(KV_WIDTH, m), lead(NSA_KV, m, LANES)),
        out_shape=(jax.ShapeDtypeStruct((2, KV_WIDTH, m), BF16),
                   jax.ShapeDtypeStruct((2, NSA_KV, m, LANES), BF16)),
        compiler_params=_cparams(("parallel",)),
        name="compress_prompt",
    )(x2, cw["pe_a"], cw["pe_b"], cw["w_a"], cw["w_b"], cw["b1"], cw["w2"], cw["w2t"], cw["w2a"])


def _compress_weights(pe, w1, b1, w2):
    half = CMP_BLOCK // 2
    eye = jnp.eye(NSA_KV, dtype=F32)
    w1r = w1.reshape(2, half, HEAD_DIM, CMP_HIDDEN)

    def big(wh):
        return jnp.einsum("rdf,pk->rpdkf", wh, eye).reshape(half * KV_WIDTH, NSA_KV * CMP_HIDDEN)

    def perow(p):
        return jnp.broadcast_to(p[:, None, :], (half, NSA_KV, HEAD_DIM)).reshape(1, half * KV_WIDTH)

    w2bd = jnp.einsum("fd,pk->pfkd", w2, eye).reshape(NSA_KV * CMP_HIDDEN, KV_WIDTH)
    w2a = jnp.stack([jnp.concatenate(
        [w2bd[:, kv * HEAD_DIM:(kv + 1) * HEAD_DIM], jnp.zeros((NSA_KV * CMP_HIDDEN, LANES - HEAD_DIM), F32)],
        axis=1) for kv in range(NSA_KV)])
    return dict(pe_a=perow(pe[:half]), pe_b=perow(pe[half:]),
                w_a=big(w1r[0]).astype(BF16), w_b=big(w1r[1]).astype(BF16),
                b1=jnp.tile(b1, NSA_KV)[None], w2=w2bd.astype(BF16), w2t=w2bd.T.astype(BF16),
                w2a=w2a.astype(BF16))


def _top_select(vals, idx, forced, axis, rounds):
    sel = forced
    for _ in range(rounds):
        mx = jnp.max(vals, axis=axis, keepdims=True)
        first = jnp.min(jnp.where(vals == mx, idx, jnp.int32(1 << 30)), axis=axis, keepdims=True)
        hit = jnp.logical_and(idx == first, mx > 0.5 * NEG)
        sel = jnp.logical_or(sel, hit)
        vals = jnp.where(hit, NEG, vals)
    return sel


def _pack_heads(o, tq):
    lane = lax.broadcasted_iota(jnp.int32, (tq, LANES), 1)
    o4 = o.reshape(NSA_GROUP, tq, LANES)
    halves = []
    for a in range(NSA_GROUP // 2):
        halves.append(jnp.where(lane < HEAD_DIM, o4[2 * a], pltpu.roll(o4[2 * a + 1], HEAD_DIM, 1)))
    return halves


def _nsa_cmp_kernel(q_ref, ltab_ref, kt_ref, v_ref, ov_ref, o_ref, sel_ref, *, tq, n_blk):
    qi = pl.program_id(0)
    rows = NSA_GROUP * tq
    n_c = kt_ref.shape[2]
    r = lax.broadcasted_iota(jnp.int32, (HEAD_DIM, n_c), 0)
    c = lax.broadcasted_iota(jnp.int32, (HEAD_DIM, n_c), 1)
    place = jnp.where(jnp.logical_and(r < 32, c == (tq // CMP_STRIDE) * qi + (r & 15) - 9), 1.0, 0.0).astype(BF16)
    qpos = qi * tq + (lax.broadcasted_iota(jnp.int32, (rows, 1), 0) & (tq - 1))
    c_max = (qpos - (CMP_BLOCK - 1)) >> 4
    mask = lax.broadcasted_iota(jnp.int32, (rows, n_c), 1) <= c_max
    imp = []
    for kv in range(NSA_KV):
        q4 = q_ref[kv * NSA_GROUP:(kv + 1) * NSA_GROUP].reshape(rows, HEAD_DIM)
        s = _dot(q4, kt_ref[kv]) + _dot(ltab_ref[kv], place)
        s = jnp.where(mask, s, NEG)
        m = jnp.max(s, axis=-1, keepdims=True)
        e = jnp.where(mask, jnp.exp(s - m), 0.0)
        p = e / jnp.maximum(jnp.sum(e, axis=-1, keepdims=True), 1e-30)
        h0, h1 = _pack_heads(_dot(p.astype(BF16), v_ref[kv]), tq)
        o_ref[:, (2 * kv) * LANES:(2 * kv + 1) * LANES] = h0
        o_ref[:, (2 * kv + 1) * LANES:(2 * kv + 2) * LANES] = h1
        p4 = p.reshape(NSA_GROUP, tq, n_c)
        imp.append(_dot((p4[0] + p4[1] + p4[2] + p4[3]).astype(BF16), ov_ref[...]))
    imp = jnp.stack(imp)
    j = lax.broadcasted_iota(jnp.int32, (NSA_KV, tq, n_blk), 2)
    q_blk = (qi * tq + lax.broadcasted_iota(jnp.int32, (NSA_KV, tq, 1), 1)) >> 6
    forced = jnp.logical_or(j == 0, jnp.logical_and(j >= q_blk - 1, j <= q_blk))
    cand = jnp.logical_and(j >= 1, j <= q_blk - 2)
    sel = _top_select(jnp.where(cand, imp, NEG), j, forced, 2, N_SELECT - N_FORCED)
    sel_ref[...] = jnp.where(sel, 1.0, 0.0).astype(BF16)


def _nsa_cmp_prompt(qn_h, ltab, cmp_kt, cmp_va, overlap, tq):
    s = qn_h.shape[1]
    n_c = cmp_kt.shape[2]
    n_blk = overlap.shape[1]
    full = lambda a: pl.BlockSpec(a.shape, lambda i: (0,) * a.ndim)
    return pl.pallas_call(
        functools.partial(_nsa_cmp_kernel, tq=tq, n_blk=n_blk),
        grid=(s // tq,),
        in_specs=[pl.BlockSpec((NSA_HEADS, tq, HEAD_DIM), lambda i: (0, i, 0)),
                  full(ltab), full(cmp_kt), full(cmp_va), full(overlap)],
        out_specs=(pl.BlockSpec((tq, NSA_WIDTH), lambda i: (i, 0)),
                   pl.BlockSpec((NSA_KV, tq, n_blk), lambda i: (0, i, 0))),
        out_shape=(jax.ShapeDtypeStruct((s, NSA_WIDTH), F32),
                   jax.ShapeDtypeStruct((NSA_KV, s, n_blk), BF16)),
        compiler_params=_cparams(("parallel",)),
        name="nsa_cmp_prompt",
    )(qn_h, ltab, cmp_kt, cmp_va, overlap)


def _nsa_sel_kernel(q_ref, kt_ref, v_ref, sel_ref, pn_ref, o_ref, *, tq, n_blk):
    qi = pl.program_id(1)
    rows = NSA_GROUP * tq
    tk_far = SEL_PAD + LANES
    q4 = q_ref[...].reshape(rows, HEAD_DIM)
    selm = sel_ref[...]

    def tile(start, tk, bias, m, acc):
        ps = pl.multiple_of(start + SEL_PAD, LANES)
        jb = lax.broadcasted_iota(jnp.int32, (n_blk, tk), 0)
        lk = lax.broadcasted_iota(jnp.int32, (n_blk, tk), 1)
        expand = jnp.where(jb == ((start + lk) >> 6), 1.0, 0.0).astype(BF16)
        drop = (_dot(selm, expand) - 1.0) * (-NEG)
        s = _dot(q4, kt_ref[:, pl.ds(ps, tk)])
        if bias is not None:
            s = s + bias
        s3 = s.reshape(NSA_GROUP, tq, tk) + drop[None]
        m_new = jnp.maximum(m, jnp.max(s3, axis=-1, keepdims=True))
        p = jnp.exp(s3 - m_new).reshape(rows, tk)
        alpha = jnp.exp(m - m_new).reshape(rows, 1)
        acc = alpha * acc + _dot(p.astype(BF16), v_ref[pl.ds(ps, tk), :])
        return m_new, acc

    near = (qi - 1) * tq
    m0 = jnp.full((NSA_GROUP, tq, 1), NEG, F32)
    m, acc = tile(near, 2 * tq, pn_ref[...], m0, jnp.zeros((rows, LANES), F32))

    def body(t, c):
        return tile(near - tk_far * (t + 1), tk_far, None, *c)

    m, acc = lax.fori_loop(0, (near + tk_far - 1) // tk_far, body, (m, acc))
    o = acc / jnp.maximum(acc[:, HEAD_DIM:HEAD_DIM + 1], 1e-30)
    h0, h1 = _pack_heads(o, tq)
    o_ref[:, 0:LANES] = h0
    o_ref[:, LANES:2 * LANES] = h1


def _nsa_sel_prompt(qn_h, ks_t, vs_a, sel, p_near, tq):
    s = qn_h.shape[1]
    sp = ks_t.shape[2]
    n_blk = sel.shape[2]
    return pl.pallas_call(
        functools.partial(_nsa_sel_kernel, tq=tq, n_blk=n_blk),
        grid=(NSA_KV, s // tq),
        in_specs=[pl.BlockSpec((NSA_GROUP, tq, HEAD_DIM), lambda k, i: (k, i, 0)),
                  pl.BlockSpec((None, HEAD_DIM, sp), lambda k, i: (k, 0, 0)),
                  pl.BlockSpec((None, sp, LANES), lambda k, i: (k, 0, 0)),
                  pl.BlockSpec((None, tq, n_blk), lambda k, i: (k, i, 0)),
                  pl.BlockSpec((None, NSA_GROUP * tq, 2 * tq), lambda k, i: (k, 0, 0))],
        out_specs=pl.BlockSpec((tq, 2 * LANES), lambda k, i: (i, k)),
        out_shape=jax.ShapeDtypeStruct((s, NSA_WIDTH), F32),
        compiler_params=_cparams(("parallel", "parallel")),
        name="nsa_sel_prompt",
    )(qn_h, ks_t, vs_a, sel, p_near)


def _nsa_win_kernel(q_ref, kt_ref, v_ref, pw_ref, o_ref, *, tq):
    qi = pl.program_id(1)
    rows = NSA_GROUP * tq
    span = WINDOW + tq
    q4 = q_ref[...].reshape(rows, HEAD_DIM)
    ps = pl.multiple_of(qi * tq, LANES)
    s = _dot(q4, kt_ref[:, pl.ds(ps, span)]) + pw_ref[...]
    real = lax.broadcasted_iota(jnp.int32, (rows, span), 1) >= WIN_PAD - qi * tq
    s = jnp.where(real, s, NEG)
    m = jnp.max(s, axis=-1, keepdims=True)
    e = jnp.where(real, jnp.exp(s - m), 0.0)
    acc = _dot(e.astype(BF16), v_ref[pl.ds(ps, span), :])
    o = acc / jnp.maximum(acc[:, HEAD_DIM:HEAD_DIM + 1], 1e-30)
    h0, h1 = _pack_heads(o, tq)
    o_ref[:, 0:LANES] = h0
    o_ref[:, LANES:2 * LANES] = h1


def _nsa_win_prompt(qn_h, kw_t, vw_a, p_win, tq):
    s = qn_h.shape[1]
    sp = kw_t.shape[2]
    return pl.pallas_call(
        functools.partial(_nsa_win_kernel, tq=tq),
        grid=(NSA_KV, s // tq),
        in_specs=[pl.BlockSpec((NSA_GROUP, tq, HEAD_DIM), lambda k, i: (k, i, 0)),
                  pl.BlockSpec((None, HEAD_DIM, sp), lambda k, i: (k, 0, 0)),
                  pl.BlockSpec((None, sp, LANES), lambda k, i: (k, 0, 0)),
                  pl.BlockSpec((None, NSA_GROUP * tq, WINDOW + tq), lambda k, i: (k, 0, 0))],
        out_specs=pl.BlockSpec((tq, 2 * LANES), lambda k, i: (i, k)),
        out_shape=jax.ShapeDtypeStruct((s, NSA_WIDTH), F32),
        compiler_params=_cparams(("parallel", "parallel")),
        name="nsa_win_prompt",
    )(qn_h, kw_t, vw_a, p_win)


def _route(logits):
    lane = lax.broadcasted_iota(jnp.int32, logits.shape, 1)
    big = jnp.int32(1 << 30)
    is_g = jnp.logical_and(lane >= N_EXPERTS, lane < N_EXPERTS + N_GROUPS)
    gl = jnp.where(is_g, logits, NEG)
    gmax = jnp.max(gl, axis=-1, keepdims=True)
    gidx = jnp.min(jnp.where(gl == gmax, lane - N_EXPERTS, big), axis=-1, keepdims=True)
    g_w = 1.0 / jnp.sum(jnp.where(is_g, jnp.exp(gl - gmax), 0.0), axis=-1, keepdims=True)
    in_g = jnp.logical_and(lane < N_EXPERTS, (lane >> 3) == gidx)
    el = jnp.where(in_g, logits, NEG)
    emax = jnp.max(el, axis=-1, keepdims=True)
    ee = jnp.where(in_g, jnp.exp(el - emax), 0.0)
    prob = jnp.where(in_g, ee / jnp.sum(ee, axis=-1, keepdims=True), -1.0)
    p1 = jnp.max(prob, axis=-1, keepdims=True)
    i1 = jnp.min(jnp.where(prob == p1, lane, big), axis=-1, keepdims=True)
    prob2 = jnp.where(lane == i1, -1.0, prob)
    p2 = jnp.max(prob2, axis=-1, keepdims=True)
    i2 = jnp.min(jnp.where(prob2 == p2, lane, big), axis=-1, keepdims=True)
    tot = p1 + p2
    return jnp.where(lane == i1, g_w * (p1 / tot), jnp.where(lane == i2, g_w * (p2 / tot), 0.0))


def _merge_kernel(x_ref, osb_ref, oc_ref, os_ref, ow_ref, gmix_ref, wg_ref, eg_ref, wbs_ref, wbn_ref,
                  wo_ref, gffn_ref, wr_ref, h_ref, hn_ref, comb_ref):
    mm = lambda a, w_ref: _dot(a.astype(BF16), w_ref[...])
    x = x_ref[...]
    u = mm(_rms(x, gmix_ref[...]), wg_ref)
    g = jax.nn.sigmoid(u[:, 0:LANES])
    g_hi = _r16(g)
    g_mid = _r16(g - g_hi)
    g_br = mm(g_hi, eg_ref) + mm(g_mid, eg_ref) + mm(g - g_hi - g_mid, eg_ref)
    o_nsa = (g_br[:, 0:NSA_WIDTH] * oc_ref[...] + g_br[:, NSA_WIDTH:2 * NSA_WIDTH] * os_ref[...]
             + g_br[:, 2 * NSA_WIDTH:3 * NSA_WIDTH] * ow_ref[...])
    merged = (jax.nn.sigmoid(u[:, LANES:LANES + D_MODEL]) * mm(osb_ref[...], wbs_ref)
              + jax.nn.sigmoid(u[:, LANES + D_MODEL:LANES + 2 * D_MODEL]) * mm(o_nsa, wbn_ref))
    h = x + mm(merged, wo_ref)
    hn = _rms(h, gffn_ref[...])
    h_ref[...] = h
    hn_ref[...] = hn.astype(BF16)
    comb_ref[...] = _route(mm(hn, wr_ref))


def _merge(x, o_sb, o_c, o_s, o_w, mw, tm):
    n = x.shape[0]
    row = lambda c: pl.BlockSpec((tm, c), lambda i: (i, 0))
    full = lambda a: pl.BlockSpec(a.shape, lambda i: (0,) * a.ndim)
    ws = (mw["g_mix"], mw["w_gate"], mw["e_g"], mw["w_b_sb"], mw["w_b_nsa"], mw["w_o"], mw["g_ffn"], mw["w_r"])
    return pl.pallas_call(
        _merge_kernel,
        grid=(n // tm,),
        in_specs=[row(D_MODEL), row(SB_WIDTH), row(NSA_WIDTH), row(NSA_WIDTH), row(NSA_WIDTH)]
                 + [full(a) for a in ws],
        out_specs=(row(D_MODEL), row(D_MODEL), row(LANES)),
        out_shape=(jax.ShapeDtypeStruct((n, D_MODEL), F32), jax.ShapeDtypeStruct((n, D_MODEL), BF16),
                   jax.ShapeDtypeStruct((n, LANES), F32)),
        compiler_params=_cparams(("parallel",)),
        name="merge",
    )(x, o_sb, o_c, o_s, o_w, *ws)


def _moe_kernel(h_ref, hn_ref, comb_ref, pe_ref, weg_ref, weu_ref, wed_ref, gple_ref, wpg_ref, wpe_ref,
                gfin_ref, y_ref, acc_ref):
    e = pl.program_id(1)

    @pl.when(e == 0)
    def _():
        acc_ref[...] = jnp.zeros_like(acc_ref)

    hn = hn_ref[...]
    lane = lax.broadcasted_iota(jnp.int32, comb_ref.shape, 1)
    ce = jnp.sum(jnp.where(lane == e, comb_ref[...], 0.0), axis=-1, keepdims=True)
    hid = jax.nn.silu(_dot(hn, weg_ref[...])) * _dot(hn, weu_ref[...]) * ce
    acc_ref[...] += _dot(hid.astype(BF16), wed_ref[...])

    @pl.when(e == N_EXPERTS - 1)
    def _():
        h = h_ref[...] + acc_ref[...]
        gate = jax.nn.sigmoid(_dot(_rms(h, gple_ref[...]).astype(BF16), wpg_ref[...]))
        h = h + gate * _dot(pe_ref[...].astype(BF16), wpe_ref[...])
        y_ref[...] = _rms(h, gfin_ref[...])


def _moe(h, hn, comb, p_emb, ew, tm):
    n = h.shape[0]
    row = lambda c: pl.BlockSpec((tm, c), lambda i, e: (i, 0))
    full = lambda a: pl.BlockSpec(a.shape, lambda i, e: (0,) * a.ndim)
    exp = lambda a, b: pl.BlockSpec((None, a, b), lambda i, e: (e, 0, 0))
    return pl.pallas_call(
        _moe_kernel,
        grid=(n // tm, N_EXPERTS),
        in_specs=[row(D_MODEL), row(D_MODEL), row(LANES), row(PLE_DIM),
                  exp(D_MODEL, EXPERT_FF), exp(D_MODEL, EXPERT_FF), exp(EXPERT_FF, D_MODEL),
                  full(ew["g_ple"]), full(ew["w_pg"]), full(ew["w_pe"]), full(ew["g_fin"])],
        out_specs=row(D_MODEL),
        out_shape=jax.ShapeDtypeStruct((n, D_MODEL), F32),
        scratch_shapes=[pltpu.VMEM((tm, D_MODEL), F32)],
        compiler_params=_cparams(("parallel", "arbitrary")),
        name="moe_ple",
    )(h, hn, comb, p_emb, ew["w_eg"], ew["w_eu"], ew["w_ed"], ew["g_ple"], ew["w_pg"], ew["w_pe"], ew["g_fin"])


def _sample1_kernel(pt_ref, qsb_ref, qp_ref, sbk_ref, sbv_ref, cck_ref, ccv_ref,
                    ssb_ref, tri_ref, snsa_ref, pea_ref, peb_ref, wa_ref, wb_ref, b1_ref, w2_ref,
                    w2a_ref, bc_ref, ovt_ref, gs_ref,
                    osb_ref, oc_ref, sel_ref,
                    x2k_ref, x2v_ref, acc_ref, run_ref, *, n_pages, n_c, n_sel):
    del pt_ref
    p = pl.program_id(1)
    page = n_pages - 1 - p

    @pl.when(p == 0)
    def _():
        acc_ref[...] = jnp.zeros_like(acc_ref)
        run_ref[...] = jnp.zeros_like(run_ref)

    z = _dot_hp(_r16(sbk_ref[...]) * _r16(qsb_ref[...]), ssb_ref[...])
    t = jnp.log(1.0 + jnp.exp(-jnp.abs(z)))
    log_keep = -(jnp.maximum(z, 0.0) + t)
    after = _dot_hp(tri_ref[...], log_keep)
    a = jnp.exp(jnp.minimum(z, 0.0) - t + after + run_ref[...])
    acc_ref[...] += _dot_tn_hp(_r16(a), _r16(sbv_ref[...]))
    run_ref[...] += jnp.sum(log_keep, axis=0, keepdims=True)

    rows = pl.ds(pl.multiple_of(page * 8, 8), 8)
    x2k_ref[rows, :] = cck_ref[...]
    x2v_ref[rows, :] = ccv_ref[...]

    @pl.when(p == n_pages - 1)
    def _():
        osb_ref[...] = acc_ref[0:8, :]
        cmp = []
        for i, x2 in enumerate((x2k_ref, x2v_ref)):
            hid = _compress_rows(x2[...], pea_ref[i], peb_ref[i], wa_ref[i], wb_ref[i], b1_ref[i], None)
            cmp.append(hid.astype(BF16))
        cmp_k = _dot(cmp[0], w2_ref[0])
        cmp_v = _dot(cmp[1], w2_ref[1])
        m_rows = cmp_k.shape[0]
        k4 = jnp.concatenate([_r16(cmp_k)] * NSA_GROUP, axis=1)
        s = _dot_hp(k4 * _r16(qp_ref[...]), snsa_ref[...]) + bc_ref[...]
        valid = lax.broadcasted_iota(jnp.int32, s.shape, 0) < n_c
        s = jnp.where(valid, s, NEG)
        m = jnp.max(s, axis=0, keepdims=True)
        e = jnp.where(valid, jnp.exp(s - m), 0.0)
        pc = e / jnp.maximum(jnp.sum(e, axis=0, keepdims=True), 1e-30)
        oc_ref[...] = _dot_tn_hp(_r16(pc), _r16(cmp_v))[0:8, :]
        imp = _dot_hp(ovt_ref[...], _r16(_dot_hp(pc, gs_ref[...])))
        j = lax.broadcasted_iota(jnp.int32, imp.shape, 0)
        forced = jnp.logical_or(j == 0, jnp.logical_and(j >= n_sel - 2, j < n_sel))
        cand = jnp.logical_and(j >= 1, j < n_sel - 2)
        sel = _top_select(jnp.where(cand, imp, NEG), j, forced, 0, N_SELECT - N_FORCED)
        sel_ref[...] = jnp.where(sel, 1.0, 0.0).reshape(sel_ref.shape)


def _sample1(pt, q_sb, q_perm, sbk, sbv, cck, ccv, consts, n_c, n_sel):
    b = q_sb.shape[0]
    n_pages = pt.shape[0] // b
    m_rows = n_pages * 8
    n_blk_pad = consts["ov_t"].shape[0]
    page_map = lambda bb, p, pt_ref: (pt_ref[bb * n_pages + n_pages - 1 - p], 0, 0)
    seq = lambda *shape: pl.BlockSpec((None,) + shape, lambda bb, p, pt_ref: (bb,) + (0,) * len(shape))
    full = lambda a: pl.BlockSpec(a.shape, lambda bb, p, pt_ref: (0,) * a.ndim)
    cs = [consts[k] for k in ("s_sb", "tri", "s_nsa", "pe_a", "pe_b", "w_a", "w_b", "b1", "w2", "w2a",
                              "b_cmp", "ov_t", "g_sum")]
    grid_spec = pltpu.PrefetchScalarGridSpec(
        num_scalar_prefetch=1,
        grid=(b, n_pages),
        in_specs=[seq(1, SB_WIDTH), seq(1, NSA_WIDTH),
                  pl.BlockSpec((None, PAGE, SB_WIDTH), page_map),
                  pl.BlockSpec((None, PAGE, SB_WIDTH), page_map),
                  pl.BlockSpec((None, 8, 16 * KV_WIDTH), page_map),
                  pl.BlockSpec((None, 8, 16 * KV_WIDTH), page_map)] + [full(a) for a in cs],
        out_specs=(seq(8, SB_WIDTH), seq(8, LANES), seq(n_blk_pad // 8, 8, LANES)),
        scratch_shapes=[pltpu.VMEM((m_rows, 16 * KV_WIDTH), F32), pltpu.VMEM((m_rows, 16 * KV_WIDTH), F32),
                        pltpu.VMEM((LANES, SB_WIDTH), F32), pltpu.VMEM((1, LANES), F32)])
    return pl.pallas_call(
        functools.partial(_sample1_kernel, n_pages=n_pages, n_c=n_c, n_sel=n_sel),
        grid_spec=grid_spec,
        out_shape=(jax.ShapeDtypeStruct((b, 8, SB_WIDTH), F32), jax.ShapeDtypeStruct((b, 8, LANES), F32),
                   jax.ShapeDtypeStruct((b, n_blk_pad // 8, 8, LANES), F32)),
        compiler_params=_cparams(("parallel", "arbitrary")),
        name="sample_pass1",
    )(pt, q_sb, q_perm, sbk, sbv, cck, ccv, *cs)


def _sample2_kernel(pt_ref, qp_ref, sk_ref, sv_ref, sel_ref, blast_ref, snsa_ref,
                    nks_ref, nvs_ref, nkw_ref, nvw_ref, wk_ref, wv_ref, bw_ref, bnew_ref,
                    os_ref, ow_ref, s_scr, v_scr, *, n_pages):
    del pt_ref
    p = pl.program_id(1)
    q_row = _r16(qp_ref[...])

    def scores(k):
        k4 = jnp.concatenate([_r16(k)] * NSA_GROUP, axis=1)
        return _dot_hp(k4 * q_row, snsa_ref[...])

    tile = sel_ref[p >> 2]
    sub = lax.broadcasted_iota(jnp.int32, tile.shape, 0)
    r0 = jnp.sum(jnp.where(sub == 2 * (p & 3), tile, 0.0), axis=0, keepdims=True)
    r1 = jnp.sum(jnp.where(sub == 2 * (p & 3) + 1, tile, 0.0), axis=0, keepdims=True)
    tok = lax.broadcasted_iota(jnp.int32, (PAGE, LANES), 0)
    keep = jnp.where(tok < SEL_BLOCK, r0, r1) > 0.5
    bias = jnp.where(p == n_pages - 1, blast_ref[...], 0.0)
    s_scr[p] = jnp.where(keep, scores(sk_ref[...]) + bias, NEG)
    v_scr[pl.ds(pl.multiple_of(p * PAGE, PAGE), PAGE), :] = sv_ref[...].astype(BF16)

    @pl.when(p == n_pages - 1)
    def _():
        first = tok == 0

        def attend(s_past, v_past, k_new_ref, v_new_ref):
            s_new = jnp.where(first, scores(jnp.where(first, k_new_ref[...], 0.0)) + bnew_ref[...], NEG)
            m = jnp.maximum(jnp.max(s_past, axis=0, keepdims=True), jnp.max(s_new, axis=0, keepdims=True))
            e_past = jnp.where(s_past > 0.5 * NEG, jnp.exp(s_past - m), 0.0)
            e_new = jnp.where(first, jnp.exp(s_new - m), 0.0)
            l = jnp.maximum(jnp.sum(e_past, axis=0, keepdims=True) + jnp.sum(e_new, axis=0, keepdims=True), 1e-30)
            o = _dot((e_past / l).T.astype(BF16), v_past)
            o = o + _r16(e_new / l).T[:, 0:1] * _r16(v_new_ref[...])
            return o[0:8, :]

        os_ref[...] = attend(s_scr[...].reshape(n_pages * PAGE, LANES), v_scr[...], nks_ref, nvs_ref)
        in_win = lax.broadcasted_iota(jnp.int32, (wk_ref.shape[0], LANES), 0) >= 1
        s_w = jnp.where(in_win, scores(wk_ref[...]) + bw_ref[...], NEG)
        ow_ref[...] = attend(s_w, wv_ref[...].astype(BF16), nkw_ref, nvw_ref)


def _sample2(pt, q_perm, sk, sv, sel, new_rows, win_k, win_v, consts):
    b = q_perm.shape[0]
    n_pages = pt.shape[0] // b
    w_rows = win_k.shape[1]
    page_map = lambda bb, p, pt_ref: (pt_ref[bb * n_pages + p], 0, 0)
    seq = lambda *shape: pl.BlockSpec((None,) + shape, lambda bb, p, pt_ref: (bb,) + (0,) * len(shape))
    full = lambda a: pl.BlockSpec(a.shape, lambda bb, p, pt_ref: (0,) * a.ndim)
    grid_spec = pltpu.PrefetchScalarGridSpec(
        num_scalar_prefetch=1,
        grid=(b, n_pages),
        in_specs=[seq(1, NSA_WIDTH),
                  pl.BlockSpec((None, PAGE, KV_WIDTH), page_map),
                  pl.BlockSpec((None, PAGE, KV_WIDTH), page_map),
                  seq(*sel.shape[1:]), full(consts["b_last"]), full(consts["s_nsa"]),
                  seq(1, KV_WIDTH), seq(1, KV_WIDTH), seq(1, KV_WIDTH), seq(1, KV_WIDTH),
                  seq(w_rows, KV_WIDTH), seq(w_rows, KV_WIDTH), full(consts["b_win"]), full(consts["b_new"])],
        out_specs=(seq(8, LANES), seq(8, LANES)),
        scratch_shapes=[pltpu.VMEM((n_pages, PAGE, LANES), F32), pltpu.VMEM((n_pages * PAGE, KV_WIDTH), BF16)])
    return pl.pallas_call(
        functools.partial(_sample2_kernel, n_pages=n_pages),
        grid_spec=grid_spec,
        out_shape=(jax.ShapeDtypeStruct((b, 8, LANES), F32), jax.ShapeDtypeStruct((b, 8, LANES), F32)),
        compiler_params=_cparams(("parallel", "arbitrary")),
        name="sample_pass2",
    )(pt, q_perm, sk, sv, sel, consts["b_last"], consts["s_nsa"], *new_rows, win_k, win_v,
      consts["b_win"], consts["b_new"])


def _sample_kernel(pt_ref, qsb_ref, qnt_ref, sbk_ref, sbv_ref, cck_ref, ccv_ref, slk_ref, slv_ref,
                   tri_ref, pe_ref, w1a_ref, w1b_ref, b1_ref, w2_ref, bc_ref, bl_ref, bw_ref, bn_ref,
                   ovt_ref, gs_ref, nks_ref, nvs_ref, nkw_ref, nvw_ref, wk_ref, wv_ref,
                   osb_ref, oc_ref, os_ref, ow_ref,
                   x2_scr, s_scr, v_scr, acc_scr, run_scr, *, n_pages, n_c, n_sel):
    del pt_ref
    p = pl.program_id(1)
    page = n_pages - 1 - p
    m_rows = n_pages * (PAGE // CMP_STRIDE)
    lane = lax.broadcasted_iota(jnp.int32, (PAGE, LANES), 1)

    @pl.when(p == 0)
    def _():
        acc_scr[...] = jnp.zeros_like(acc_scr)
        run_scr[...] = jnp.zeros_like(run_scr)

    def nsa_scores(k0, k1):
        return _dot(k0.astype(BF16), qnt_ref[0]) + _dot(k1.astype(BF16), qnt_ref[1])

    def per_kv(ref, rows):
        return [ref[pl.ds(kv, rows, stride=NSA_KV), :] for kv in range(NSA_KV)]

    run = run_scr[...]
    alive = jnp.max(jnp.where(lane[0:1] < SB_HEADS, run, NEG)) > EXP_UNDERFLOW

    @pl.when(alive)
    def _():
        qh = _r16(qsb_ref[...])
        z = jnp.zeros((PAGE, LANES), F32)
        for h in range(SB_HEADS):
            kh = _r16(sbk_ref[pl.ds(h, PAGE, stride=SB_HEADS), :])
            z = jnp.where(lane == h, jnp.sum(kh * qh[h:h + 1, :], axis=1, keepdims=True), z)
        t = jnp.log(1.0 + jnp.exp(-jnp.abs(z)))
        log_keep = -(jnp.maximum(z, 0.0) + t)
        after = _dot_hp(tri_ref[...], log_keep)
        a = _r16(jnp.exp(jnp.minimum(z, 0.0) - t + after + run))
        for h in range(SB_HEADS):
            vh = _r16(sbv_ref[pl.ds(h, PAGE, stride=SB_HEADS), :])
            acc_scr[h:h + 1, :] += jnp.sum(a[:, h:h + 1] * vh, axis=0, keepdims=True)
        run_scr[...] = run + jnp.sum(log_keep, axis=0, keepdims=True)

    crow = pl.multiple_of(page * (PAGE // CMP_STRIDE), 8)
    for i, ref in enumerate((cck_ref, ccv_ref)):
        for kv in range(NSA_KV):
            for r2 in range(CMP_STRIDE // 2):
                pair = [ref[pl.ds(NSA_KV * r + kv, PAGE // CMP_STRIDE, stride=NSA_KV * CMP_STRIDE), :]
                        for r in (2 * r2, 2 * r2 + 1)]
                x2_scr[i, pl.ds(kv * m_rows + crow, PAGE // CMP_STRIDE), r2 * LANES:(r2 + 1) * LANES] = (
                    jnp.concatenate(pair, axis=1))

    s_scr[page] = nsa_scores(*per_kv(slk_ref, PAGE))
    trow = pl.multiple_of(page * PAGE, PAGE)
    for kv, v in enumerate(per_kv(slv_ref, PAGE)):
        v_scr[kv, pl.ds(trow, PAGE), :] = v.astype(BF16)

    @pl.when(p == n_pages - 1)
    def _():
        osb_ref[...] = acc_scr[...]
        row8 = lax.broadcasted_iota(jnp.int32, (8, HEAD_DIM), 0)
        rr = lax.broadcasted_iota(jnp.int32, (LANES, LANES), 0)
        cc = lax.broadcasted_iota(jnp.int32, (LANES, LANES), 1)

        def weighted(prob, v0, v1):
            pt = prob.T.astype(BF16)
            return _dot(pt, v0.astype(BF16)), _dot(pt, v1.astype(BF16))

        def pick(o0, o1):
            return jnp.where(row8 < NSA_GROUP, o0[0:8], o1[0:8])

        cmp = []
        for i in range(2):
            x2 = x2_scr[i]
            a = _dot((x2 + pe_ref[i, 0:1, :]).astype(BF16), w1a_ref[i])
            b = _dot((x2 + pe_ref[i, 1:2, :]).astype(BF16), w1b_ref[i])
            b = pltpu.roll(b, NSA_KV * m_rows - 1, 0)
            cmp.append(_dot(jax.nn.gelu(a + b + b1_ref[i]).astype(BF16), w2_ref[i]))
        ck, cv = cmp
        s = nsa_scores(ck[0:m_rows], ck[m_rows:]) + bc_ref[...]
        valid = lax.broadcasted_iota(jnp.int32, s.shape, 0) < n_c
        s = jnp.where(valid, s, NEG)
        e = jnp.where(valid, jnp.exp(s - jnp.max(s, axis=0, keepdims=True)), 0.0)
        pc = e / jnp.maximum(jnp.sum(e, axis=0, keepdims=True), 1e-30)
        oc_ref[...] = pick(*weighted(pc, cv[0:m_rows], cv[m_rows:]))
        imp = _dot_hp(ovt_ref[...], _r16(_dot_hp(pc, gs_ref[...])))
        j = lax.broadcasted_iota(jnp.int32, imp.shape, 0)
        forced = jnp.logical_or(j == 0, jnp.logical_and(j >= n_sel - 2, j < n_sel))
        cand = jnp.logical_and(j >= 1, j < n_sel - 2)
        sel = _top_select(jnp.where(cand, imp, NEG), j, forced, 0, N_SELECT - N_FORCED)

        def attend(s_past, v0, v1, k_new_ref, v_new_ref):
            kn = [jnp.broadcast_to(k_new_ref[kv:kv + 1, :], (16, HEAD_DIM)) for kv in range(NSA_KV)]
            s_new = nsa_scores(*kn)[0:1] + bn_ref[...]
            m = jnp.maximum(jnp.max(s_past, axis=0, keepdims=True), s_new)
            e_past = jnp.where(s_past > 0.5 * NEG, jnp.exp(s_past - m), 0.0)
            e_new = jnp.exp(s_new - m)
            l = jnp.maximum(jnp.sum(e_past, axis=0, keepdims=True) + e_new, 1e-30)
            o0, o1 = weighted(e_past / l, v0, v1)
            p_new = jnp.sum(jnp.where(rr == cc, _r16(e_new / l), 0.0), axis=1, keepdims=True)
            vn = _r16(v_new_ref[...])
            return pick(o0 + p_new * vn[0:1], o1 + p_new * vn[1:2])

        pg = lax.broadcasted_iota(jnp.int32, (n_pages, PAGE, LANES), 0)
        s_all = s_scr[...] + jnp.where(pg == n_pages - 1, bl_ref[...][None], 0.0)
        blocks = 2 * n_pages
        keep = jnp.broadcast_to(jnp.where(sel, 1.0, 0.0)[0:blocks][:, None, :], (blocks, SEL_BLOCK, LANES))
        s_all = jnp.where(keep.reshape(n_pages * PAGE, LANES) > 0.5, s_all.reshape(n_pages * PAGE, LANES), NEG)
        os_ref[...] = attend(s_all, v_scr[0], v_scr[1], nks_ref, nvs_ref)
        w_rows = wk_ref.shape[0] // NSA_KV
        in_win = lax.broadcasted_iota(jnp.int32, (w_rows, LANES), 0) >= 1
        s_w = jnp.where(in_win, nsa_scores(*per_kv(wk_ref, w_rows)) + bw_ref[...], NEG)
        ow_ref[...] = attend(s_w, *per_kv(wv_ref, w_rows), nkw_ref, nvw_ref)


def _sample(pt, q_sb, qn_t, caches, new_rows, win_k, win_v, consts, n_c, n_sel):
    b = q_sb.shape[0]
    n_pages = pt.shape[0] // b
    m_rows = n_pages * (PAGE // CMP_STRIDE)
    page_map = lambda bb, p, pt_ref: (pt_ref[bb * n_pages + n_pages - 1 - p], 0, 0)
    seq = lambda *shape: pl.BlockSpec((None,) + shape, lambda bb, p, pt_ref: (bb,) + (0,) * len(shape))
    full = lambda a: pl.BlockSpec(a.shape, lambda bb, p, pt_ref: (0,) * a.ndim)
    paged = lambda a: pl.BlockSpec((None,) + a.shape[1:], page_map)
    cs = [consts[k] for k in ("tri", "pe", "w1a", "w1b", "b1", "w2", "b_cmp", "b_last", "b_win", "b_new",
                              "ov_t", "g_sum")]
    out = jax.ShapeDtypeStruct((b, 8, HEAD_DIM), F32)
    grid_spec = pltpu.PrefetchScalarGridSpec(
        num_scalar_prefetch=1,
        grid=(b, n_pages),
        in_specs=[seq(SB_HEADS, HEAD_DIM), seq(NSA_KV, HEAD_DIM, LANES)] + [paged(a) for a in caches]
                 + [full(a) for a in cs] + [seq(NSA_KV, HEAD_DIM)] * 4
                 + [seq(*win_k.shape[1:]), seq(*win_v.shape[1:])],
        out_specs=(seq(8, HEAD_DIM),) * 4,
        scratch_shapes=[pltpu.VMEM((2, NSA_KV * m_rows, CMP_STRIDE * HEAD_DIM), F32),
                        pltpu.VMEM((n_pages, PAGE, LANES), F32),
                        pltpu.VMEM((NSA_KV, n_pages * PAGE, HEAD_DIM), BF16),
                        pltpu.VMEM((SB_HEADS, HEAD_DIM), F32), pltpu.VMEM((1, LANES), F32)])
    return pl.pallas_call(
        functools.partial(_sample_kernel, n_pages=n_pages, n_c=n_c, n_sel=n_sel),
        grid_spec=grid_spec,
        out_shape=(out,) * 4,
        compiler_params=_cparams(("parallel", "arbitrary")),
        name="sample_mix",
    )(pt, q_sb, qn_t, *caches, *cs, *new_rows, win_k, win_v)


PAGES_PER_STEP = 8


def _decode_kernel(pt_ref, qcol_ref, qrow_ref, qz_ref, sbk_hbm, sbv_hbm, *refs, n_pages, n_c, n_sel):
    pps = PAGES_PER_STEP
    pages = [refs[a * pps:(a + 1) * pps] for a in range(4)]
    (tri_ref, pe_ref, w1a_ref, w1b_ref, b1_ref, w2_ref, bc_ref, bs_ref, bw_ref, bn_ref, ov_ref, ex_ref,
     nks_ref, nvs_ref, nkw_ref, nvw_ref, wk_ref, wv_ref,
     osb_ref, oc_ref, os_ref, ow_ref,
     x2_scr, stage_scr, s_scr, vt_scr, acc_scr, run_scr, sbk_buf, sbv_buf, sb_sem) = refs[4 * pps:]
    p = pl.program_id(1)
    seq_pages = pl.program_id(0) * n_pages
    m_rows = n_pages * (PAGE // CMP_STRIDE)
    per_page = PAGE // CMP_STRIDE
    row8 = lax.broadcasted_iota(jnp.int32, (8, HEAD_DIM), 0)
    first_kv = row8 < NSA_GROUP

    @pl.when(p == 0)
    def _():
        acc_scr[...] = jnp.zeros_like(acc_scr)
        run_scr[...] = jnp.zeros_like(run_scr)

    def nsa_scores(kt0, kt1):
        return (_dot(qz_ref[0], kt0.astype(BF16)) + _dot(qz_ref[1], kt1.astype(BF16)))[0:8]

    def pad16(a):
        return jnp.concatenate([a, jnp.zeros_like(a)], axis=0).astype(BF16)

    def by_kv(a0, a1):
        return jnp.where(first_kv, a0[0:8], a1[0:8])

    alive = jnp.max(run_scr[...]) > EXP_UNDERFLOW

    def sb_copies():
        cps = []
        for u in range(pps):
            src = pt_ref[seq_pages + n_pages - 1 - (p * pps + u)]
            cps.append(pltpu.make_async_copy(sbk_hbm.at[src], sbk_buf.at[u], sb_sem.at[0, u]))
            cps.append(pltpu.make_async_copy(sbv_hbm.at[src], sbv_buf.at[u], sb_sem.at[1, u]))
        return cps

    @pl.when(alive)
    def _():
        for cp in sb_copies():
            cp.start()

    for u in range(pps):
        page = n_pages - 1 - (p * pps + u)
        cck_ref, ccv_ref, slk_ref, slv_ref = (pages[a][u] for a in range(4))
        crow = pl.multiple_of(page * per_page, per_page)
        for i, ref in enumerate((cck_ref, ccv_ref)):
            for kv in range(NSA_KV):
                stage = stage_scr.at[(u * 2 + i) * NSA_KV + kv]
                stage[...] = ref[kv].T
                for r2 in range(CMP_STRIDE // 2):
                    pair = [stage[pl.ds(r, per_page, stride=CMP_STRIDE), :] for r in (2 * r2, 2 * r2 + 1)]
                    x2_scr[i, pl.ds(kv * m_rows + crow, per_page), r2 * LANES:(r2 + 1) * LANES] = (
                        jnp.concatenate(pair, axis=1))

        col = pl.ds(pl.multiple_of(page * PAGE, PAGE), PAGE)
        s_scr[:, col] = nsa_scores(slk_ref[0], slk_ref[1])
        for kv in range(NSA_KV):
            vt_scr[kv, :, col] = slv_ref[kv].astype(BF16)

    @pl.when(alive)
    def _():
        for cp in sb_copies():
            cp.wait()
        run = run_scr[...]
        o = jnp.zeros((8, HEAD_DIM), F32)
        for u in range(pps):
            sbk_ref, sbv_ref = sbk_buf.at[u], sbv_buf.at[u]
            z = jnp.concatenate(
                [jnp.sum(_r16(sbk_ref[h]) * _r16(qcol_ref[h]), axis=0, keepdims=True) for h in range(SB_HEADS)],
                axis=0)
            t = jnp.log(1.0 + jnp.exp(-jnp.abs(z)))
            log_keep = -(jnp.maximum(z, 0.0) + t)
            after = _dot_hp(log_keep, tri_ref[...])
            a = _r16(jnp.exp(jnp.minimum(z, 0.0) - t + after + run))
            for h in range(SB_HEADS):
                oh = lax.dot_general(a, _r16(sbv_ref[h]), (((1,), (1,)), ((), ())),
                                     preferred_element_type=F32, precision=HIGHEST)
                o = o + jnp.where(row8 == h, oh, 0.0)
            run = run + jnp.sum(log_keep, axis=1, keepdims=True)
        acc_scr[...] += o
        run_scr[...] = run

    @pl.when(p == n_pages // pps - 1)
    def _():
        osb_ref[...] = acc_scr[...]
        cmp = []
        for i in range(2):
            x2 = x2_scr[i]
            a = _dot((x2 + pe_ref[i, 0:1, :]).astype(BF16), w1a_ref[i])
            b = _dot((x2 + pe_ref[i, 1:2, :]).astype(BF16), w1b_ref[i])
            b = pltpu.roll(b, NSA_KV * m_rows - 1, 0)
            cmp.append(_dot(jax.nn.gelu(a + b + b1_ref[i]).astype(BF16), w2_ref[i]))
        ck, cv = cmp
        s = (_dot_nt(qz_ref[0], ck[0:m_rows].astype(BF16))
             + _dot_nt(qz_ref[1], ck[m_rows:].astype(BF16)))[0:8] + bc_ref[...]
        valid = lax.broadcasted_iota(jnp.int32, s.shape, 1) < n_c
        s = jnp.where(valid, s, NEG)
        e = jnp.where(valid, jnp.exp(s - jnp.max(s, axis=1, keepdims=True)), 0.0)
        pc = e / jnp.maximum(jnp.sum(e, axis=1, keepdims=True), 1e-30)
        pc16 = pad16(pc)
        oc_ref[...] = by_kv(_dot(pc16, cv[0:m_rows].astype(BF16)), _dot(pc16, cv[m_rows:].astype(BF16)))
        g_first = lax.broadcasted_iota(jnp.int32, pc.shape, 0) < NSA_GROUP
        pg = jnp.where(g_first, jnp.sum(jnp.where(g_first, pc, 0.0), axis=0, keepdims=True),
                       jnp.sum(jnp.where(g_first, 0.0, pc), axis=0, keepdims=True))
        imp = _dot_hp(_r16(pg), ov_ref[...])
        j = lax.broadcasted_iota(jnp.int32, imp.shape, 1)
        forced = jnp.logical_or(j == 0, jnp.logical_and(j >= n_sel - 2, j < n_sel))
        cand = jnp.logical_and(j >= 1, j < n_sel - 2)
        sel = _top_select(jnp.where(cand, imp, NEG), j, forced, 1, N_SELECT - N_FORCED)

        def attend(s_past, vt0, vt1, k_new_ref, v_new_ref):
            kn, vn = _r16(k_new_ref[...]), _r16(v_new_ref[...])
            s_new = jnp.sum(_r16(qrow_ref[...]) * jnp.where(first_kv, kn[0:1], kn[1:2]), axis=1, keepdims=True)
            s_new = s_new + bn_ref[...]
            m = jnp.maximum(jnp.max(s_past, axis=1, keepdims=True), s_new)
            e_past = jnp.where(s_past > 0.5 * NEG, jnp.exp(s_past - m), 0.0)
            e_new = jnp.exp(s_new - m)
            l = jnp.maximum(jnp.sum(e_past, axis=1, keepdims=True) + e_new, 1e-30)
            p16 = pad16(e_past / l)
            o = by_kv(_dot_nt(p16, vt0), _dot_nt(p16, vt1))
            return o + _r16(e_new / l) * jnp.where(first_kv, vn[0:1], vn[1:2])

        keep = _dot(pad16(jnp.where(sel, 1.0, 0.0)), ex_ref[...])[0:8] > 0.5
        s_all = jnp.where(keep, s_scr[...] + bs_ref[...], NEG)
        os_ref[...] = attend(s_all, vt_scr[0], vt_scr[1], nks_ref, nvs_ref)
        in_win = lax.broadcasted_iota(jnp.int32, (8, wk_ref.shape[2]), 1) >= 1
        s_w = jnp.where(in_win, nsa_scores(wk_ref[0], wk_ref[1]) + bw_ref[...], NEG)
        ow_ref[...] = attend(s_w, wv_ref[0].astype(BF16), wv_ref[1].astype(BF16), nkw_ref, nvw_ref)


def _decode(pt, q_col, q_row, q_z, caches, new_rows, win_k, win_v, consts, n_c, n_sel):
    b = q_row.shape[0]
    n_pages = pt.shape[0] // b
    pps = PAGES_PER_STEP
    assert n_pages % pps == 0
    m_rows = n_pages * (PAGE // CMP_STRIDE)

    def page_map(u):
        return lambda bb, p, pt_ref: (pt_ref[bb * n_pages + n_pages - 1 - (p * pps + u)], 0, 0, 0)

    seq = lambda a: pl.BlockSpec((None,) + a.shape[1:], lambda bb, p, pt_ref: (bb,) + (0,) * (a.ndim - 1))
    full = lambda a: pl.BlockSpec(a.shape, lambda bb, p, pt_ref: (0,) * a.ndim)
    sb_caches, caches = caches[:2], caches[2:]
    in_hbm = pl.BlockSpec(memory_space=pl.ANY)
    paged = [pl.BlockSpec((None,) + a.shape[1:], page_map(u)) for a in caches for u in range(pps)]
    paged_args = [a for a in caches for _ in range(pps)]
    sb_page = (pps,) + sb_caches[0].shape[1:]
    cs = [consts[k] for k in ("tri", "pe", "w1a", "w1b", "b1", "w2", "b_cmp", "b_sel", "b_win", "b_new",
                              "ov", "expand")]
    per_seq = [*new_rows, win_k, win_v]
    out = jax.ShapeDtypeStruct((b, 8, HEAD_DIM), F32)
    grid_spec = pltpu.PrefetchScalarGridSpec(
        num_scalar_prefetch=1,
        grid=(b, n_pages // pps),
        in_specs=[seq(q_col), seq(q_row), seq(q_z), in_hbm, in_hbm] + paged + [full(a) for a in cs]
                 + [seq(a) for a in per_seq],
        out_specs=(pl.BlockSpec((None, 8, HEAD_DIM), lambda bb, p, pt_ref: (bb, 0, 0)),) * 4,
        scratch_shapes=[pltpu.VMEM((2, NSA_KV * m_rows, CMP_STRIDE * HEAD_DIM), F32),
                        pltpu.VMEM((pps * 2 * NSA_KV, PAGE, HEAD_DIM), F32),
                        pltpu.VMEM((8, n_pages * PAGE), F32),
                        pltpu.VMEM((NSA_KV, HEAD_DIM, n_pages * PAGE), BF16),
                        pltpu.VMEM((8, HEAD_DIM), F32), pltpu.VMEM((8, 1), F32),
                        pltpu.VMEM(sb_page, F32), pltpu.VMEM(sb_page, F32), pltpu.SemaphoreType.DMA((2, pps))])
    return pl.pallas_call(
        functools.partial(_decode_kernel, n_pages=n_pages, n_c=n_c, n_sel=n_sel),
        grid_spec=grid_spec,
        out_shape=(out,) * 4,
        compiler_params=_cparams(("parallel", "arbitrary")),
        name="decode_mix",
    )(pt, q_col, q_row, q_z, *sb_caches, *paged_args, *cs, *per_seq)


def _onehot_rows(delta, dist):
    oh = (np.arange(MAX_DISTANCE)[:, None] == np.asarray(dist)[None, :]).astype(np.float32)
    return jnp.dot(delta, jnp.asarray(oh), precision=HIGHEST)


def _bias_delta(table):
    bucket = _t5_bucket_np(np.arange(MAX_DISTANCE))
    oh = (np.arange(N_BUCKETS)[:, None] == bucket[None, :]).astype(np.float32)
    return jnp.dot((table - table[N_BUCKETS - 1][None]).T, jnp.asarray(oh), precision=HIGHEST)


def _toeplitz(f, n, w):
    lf = n + w - 1
    x = jnp.broadcast_to(f[:, None, :], (f.shape[0], n, lf))
    x = jnp.pad(x, ((0, 0), (0, 0), (0, 1))).reshape(f.shape[0], n * (lf + 1))[:, :n * lf]
    return x.reshape(f.shape[0], n, lf)[:, :, n - 1:n - 1 + w]


def _pad_lanes(a, n=LANES):
    return jnp.concatenate([a, jnp.zeros(a.shape[:-1] + (n - a.shape[-1],), a.dtype)], axis=-1)


def kernel(x_prompt, x_sample, cache_sb_k, cache_sb_v, cache_cmp_k, cache_cmp_v, cache_sel_k, cache_sel_v, state_win_k, state_win_v, page_table, p_prompt, p_sample, norm_mix, w_in, cmp_pe_k, cmp_w1_k, cmp_b1_k, cmp_w2_k, cmp_pe_v, cmp_w1_v, cmp_b1_v, cmp_w2_v, rel_bias_table, w_branch_sb, w_branch_nsa, w_out, norm_ffn, w_router_group, w_router_expert, w_exp_gate, w_exp_up, w_exp_down, norm_ple, w_ple_gate, w_ple, norm_final):
    assert w_in.shape[0] == 1 and x_prompt.shape[0] == 1 and x_sample.shape[1] == 1
    seq = x_prompt.shape[1]
    n_seq = x_sample.shape[0]
    n_pool = cache_sb_k.shape[1]
    n_pages = page_table.shape[1]
    past = n_pages * PAGE
    w_buf = state_win_k.shape[2]
    tq = 128
    assert seq % 512 == 0 and w_buf == WINDOW and past >= WINDOW and seq >= WINDOW

    scale = HEAD_DIM ** -0.5
    w = w_in[0]
    o_g = 2048 + 6 * KV_WIDTH
    w_q_sb, w_k_sb, w_v_sb, w_q_n = (w[:, i * 512:(i + 1) * 512] for i in range(4))
    w_kv6 = w[:, 2048:o_g]
    w_vs, w_vw = w[:, 2048 + 3 * KV_WIDTH:2048 + 4 * KV_WIDTH], w[:, 2048 + 5 * KV_WIDTH:o_g]
    w_ks, w_kw = w[:, 2048 + 2 * KV_WIDTH:2048 + 3 * KV_WIDTH], w[:, 2048 + 4 * KV_WIDTH:2048 + 5 * KV_WIDTH]
    aug = lambda wv, kv: _pad_lanes(wv[:, kv * HEAD_DIM:(kv + 1) * HEAD_DIM])
    w_all = jnp.concatenate([w_q_sb * scale, w_k_sb, w_v_sb, w_q_n * scale, w_kv6,
                             aug(w_vs, 0), aug(w_vs, 1), aug(w_vw, 0), aug(w_vw, 1)], axis=1).astype(BF16)
    w_t = jnp.concatenate([w_k_sb.T, w_ks.T, w_kw.T], axis=0).astype(BF16)
    g_mix = norm_mix[0][None]
    w_gate_f = jnp.concatenate([_pad_lanes(w[:, o_g:o_g + 3 * NSA_HEADS]), w[:, o_g + 3 * NSA_HEADS:]], axis=1)
    e_g = np.zeros((LANES, 3 * NSA_WIDTH), np.float32)
    for h in range(NSA_HEADS):
        for br in range(3):
            e_g[h * 3 + br, br * NSA_WIDTH + h * HEAD_DIM:br * NSA_WIDTH + (h + 1) * HEAD_DIM] = 1.0
    w_r = jnp.concatenate([w_router_expert[0], w_router_group[0],
                           jnp.zeros((D_MODEL, LANES - N_EXPERTS - N_GROUPS), F32)], axis=1)
    mw = dict(g_mix=g_mix, w_gate=w_gate_f.astype(BF16), e_g=jnp.asarray(e_g, BF16),
              w_b_sb=w_branch_sb[0].astype(BF16), w_b_nsa=w_branch_nsa[0].astype(BF16),
              w_o=w_out[0].astype(BF16), g_ffn=norm_ffn[0][None], w_r=w_r.astype(BF16))
    ew = dict(w_eg=w_exp_gate[0].astype(BF16), w_eu=w_exp_up[0].astype(BF16), w_ed=w_exp_down[0].astype(BF16),
              g_ple=norm_ple[0][None], w_pg=w_ple_gate[0].astype(BF16), w_pe=w_ple[0].astype(BF16),
              g_fin=norm_final[None])
    cw_k = _compress_weights(cmp_pe_k[0], cmp_w1_k[0], cmp_b1_k[0], cmp_w2_k[0])
    cw_v = _compress_weights(cmp_pe_v[0], cmp_w1_v[0], cmp_b1_v[0], cmp_w2_v[0])
    cw = {k: jnp.stack([cw_k[k], cw_v[k]]) for k in cw_k}

    delta = _bias_delta(rel_bias_table)
    rows = NSA_GROUP * tq
    by_kv = lambda a: a.reshape((NSA_KV, rows) + a.shape[2:])
    d_c = (np.arange(tq)[:, None] - CMP_STRIDE * (np.arange(16)[None] - 9) - (CMP_BLOCK - 1)).reshape(-1)
    l_tab = by_kv(_onehot_rows(delta, d_c).reshape(NSA_HEADS, tq, 16))
    l_hi = l_tab.astype(BF16)
    l_lo = (l_tab - l_hi.astype(F32)).astype(BF16)
    l_tab = jnp.concatenate([l_hi, l_lo, jnp.zeros((NSA_KV, rows, HEAD_DIM - 32), BF16)], axis=-1)
    d_n = 2 * tq - 1 - np.arange(3 * tq - 1)
    f_n = _onehot_rows(delta, d_n) + jnp.asarray(np.where(d_n < 0, NEG, 0.0), F32)[None]
    p_near = by_kv(_toeplitz(f_n, tq, 2 * tq))
    d_w = WINDOW + tq - 1 - np.arange(WINDOW + 2 * tq - 1)
    f_w = _onehot_rows(delta, d_w) + jnp.asarray(np.where((d_w < 0) | (d_w >= WINDOW), NEG, 0.0), F32)[None]
    p_win = by_kv(_toeplitz(f_w, tq, WINDOW + tq))

    xp = x_prompt[0]
    (q_sb_h, k_sb, v_sb, v_sb_b, q_n_h, k_c, v_c, k_s, v_s, k_w, v_w, vs_a, vw_a,
     k_sb_t, k_s_t, k_w_t) = _proj(xp, g_mix, w_all, w_t, 512)
    o_sb = _sb_prompt(q_sb_h, k_sb_t, v_sb_b, 256)

    x2 = jnp.stack([k_c, v_c]).reshape(2, seq // CMP_STRIDE, CMP_STRIDE * KV_WIDTH)
    cmp_t, cmp_a = _compress_prompt(x2, cw)
    n_cp = seq // CMP_STRIDE
    n_blk = seq // SEL_BLOCK
    c_start = np.arange(n_cp)[:, None] * CMP_STRIDE
    b_start = np.arange(n_blk)[None] * SEL_BLOCK
    overlap = ((c_start < b_start + SEL_BLOCK) & (c_start + CMP_BLOCK > b_start)
               & (np.arange(n_cp)[:, None] < n_cp - 1))
    overlap = jnp.asarray(overlap.astype(np.float32), BF16)
    o_c, sel = _nsa_cmp_prompt(q_n_h, l_tab, cmp_t[0].reshape(NSA_KV, HEAD_DIM, n_cp), cmp_a[1], overlap, tq)

    front = lambda a, n, axis: jnp.pad(a, [(n, 0) if ax == axis else (0, 0) for ax in range(a.ndim)])
    o_s = _nsa_sel_prompt(q_n_h, front(k_s_t.reshape(NSA_KV, HEAD_DIM, seq), SEL_PAD, 2),
                          front(vs_a, SEL_PAD, 1), sel, p_near, tq)
    o_w = _nsa_win_prompt(q_n_h, front(k_w_t.reshape(NSA_KV, HEAD_DIM, seq), WIN_PAD, 2),
                          front(vw_a, WIN_PAD, 1), p_win, tq)
    h_p, hn_p, comb_p = _merge(xp, o_sb, o_c, o_s, o_w, mw, 512)
    y_prompt = _moe(h_p, hn_p, comb_p, p_prompt[0, 0], ew, min(1024, seq))[None]

    w_s = jnp.concatenate([w_q_sb * scale, w_k_sb, w_v_sb, w_q_n * scale, w_kv6], axis=1).astype(BF16)
    xs = x_sample[:, 0]
    u = _proj_s(xs, g_mix, w_s)
    q_sb_s, k_sb_s, v_sb_s, q_n_s = (u[:, i * 512:(i + 1) * 512] for i in range(4))
    kc_s, vc_s, ks_s, vs_s, kw_s, vw_s = (u[:, 2048 + i * KV_WIDTH:2048 + (i + 1) * KV_WIDTH] for i in range(6))
    q_g = q_n_s.reshape(n_seq, NSA_KV, NSA_GROUP, HEAD_DIM)
    q_z = jnp.stack([jnp.pad(q_g[:, kv], ((0, 0), (kv * NSA_GROUP, 16 - (kv + 1) * NSA_GROUP), (0, 0)))
                     for kv in range(NSA_KV)], axis=1).astype(BF16)

    n_c = (past + 1 - CMP_BLOCK) // CMP_STRIDE + 1
    n_sel = -(-(past + 1) // SEL_BLOCK)
    m_rows = past // CMP_STRIDE
    n_blk_pad = -(-n_sel // LANES) * LANES
    tri_s = np.tril(np.ones((PAGE, PAGE), np.float32), -1)
    ov = np.zeros((m_rows, n_blk_pad), np.float32)
    cs = np.arange(n_c) * CMP_STRIDE
    for jb in range(n_sel):
        ov[:n_c, jb] = (cs < (jb + 1) * SEL_BLOCK) & (cs + CMP_BLOCK > jb * SEL_BLOCK)
    expand = jnp.arange(n_blk_pad)[:, None] == jnp.arange(past)[None, :] // SEL_BLOCK
    b_cmp = _onehot_rows(delta, past - (np.arange(m_rows) * CMP_STRIDE + CMP_BLOCK - 1))
    b_sel = jnp.concatenate([jnp.zeros((NSA_HEADS, past - PAGE), F32),
                             _onehot_rows(delta, PAGE - np.arange(PAGE))], axis=1)
    b_win = _onehot_rows(delta, w_buf - np.arange(w_buf))
    b_new = _onehot_rows(delta, np.zeros((1,), np.int64))
    half = CMP_BLOCK // 2
    cmp_p = ((cmp_pe_k[0], cmp_w1_k[0], cmp_b1_k[0], cmp_w2_k[0]), (cmp_pe_v[0], cmp_w1_v[0], cmp_b1_v[0], cmp_w2_v[0]))
    stack = lambda f, dt=F32: jnp.stack([f(*c) for c in cmp_p]).astype(dt)
    consts = dict(tri=jnp.asarray(tri_s),
                  pe=stack(lambda pe, w1, b1, w2: pe.reshape(2, half * HEAD_DIM)),
                  w1a=stack(lambda pe, w1, b1, w2: w1[:half * HEAD_DIM], BF16),
                  w1b=stack(lambda pe, w1, b1, w2: w1[half * HEAD_DIM:], BF16),
                  b1=stack(lambda pe, w1, b1, w2: b1[None]), w2=stack(lambda pe, w1, b1, w2: w2, BF16),
                  b_cmp=b_cmp, b_sel=b_sel, b_win=b_win, b_new=b_new,
                  ov=jnp.asarray(ov), expand=jnp.asarray(expand, BF16))

    native = lambda c: jnp.transpose(c[0], (0, 2, 3, 1))
    caches = [native(c) for c in (cache_sb_k, cache_sb_v, cache_cmp_k, cache_cmp_v, cache_sel_k, cache_sel_v)]
    new_rows = [a.reshape(n_seq, NSA_KV, HEAD_DIM) for a in (ks_s, vs_s, kw_s, vw_s)]
    o4 = _decode(page_table.reshape(-1), q_sb_s.reshape(n_seq, SB_HEADS, HEAD_DIM, 1),
                 q_n_s.reshape(n_seq, NSA_HEADS, HEAD_DIM), q_z, caches, new_rows,
                 native(state_win_k), native(state_win_v), consts, n_c, n_sel)
    o_sb_s, o_c_s, o_s_s, o_w_s = (o.reshape(n_seq, SB_WIDTH) for o in o4)
    h_s, hn_s, comb_s = _merge(xs, o_sb_s, o_c_s, o_s_s, o_w_s, mw, n_seq)
    y_sample = _moe(h_s, hn_s, comb_s, p_sample[0, :, 0], ew, n_seq)[:, None]

    hd = lambda a, n: a.reshape((1, a.shape[0]) + (() if a.ndim == 2 else ()) + (n, HEAD_DIM))
    pr = lambda a, n: a.reshape(1, 1, seq, n, HEAD_DIM)
    sm = lambda a, n: a.reshape(1, n_seq, 1, n, HEAD_DIM)
    win_p = lambda a: a[seq - WINDOW:].reshape(1, 1, WINDOW, NSA_KV, HEAD_DIM)
    win_s = lambda st, new: jnp.concatenate([st[0], new.reshape(n_seq, 1, NSA_KV, HEAD_DIM)], axis=1)[None, :, 1:]
    return (y_prompt, y_sample,
            pr(k_sb, SB_HEADS), pr(v_sb, SB_HEADS), pr(k_c, NSA_KV), pr(v_c, NSA_KV), pr(k_s, NSA_KV),
            pr(v_s, NSA_KV), win_p(k_w), win_p(v_w),
            sm(k_sb_s, SB_HEADS), sm(v_sb_s, SB_HEADS), sm(kc_s, NSA_KV), sm(vc_s, NSA_KV), sm(ks_s, NSA_KV),
            sm(vs_s, NSA_KV), win_s(state_win_k, kw_s), win_s(state_win_v, vw_s))
```

```python
import functools
import math

import numpy as np
import jax
import jax.numpy as jnp
from jax import lax
from jax.experimental import pallas as pl
from jax.experimental.pallas import tpu as pltpu

F32 = jnp.float32
BF16 = jnp.bfloat16
HIGHEST = lax.Precision.HIGHEST

D_MODEL = 1024
HEAD_DIM = 64
SB_HEADS = 8
NSA_HEADS = 8
NSA_KV = 2
NSA_GROUP = NSA_HEADS // NSA_KV
SB_WIDTH = SB_HEADS * HEAD_DIM
NSA_WIDTH = NSA_HEADS * HEAD_DIM
KV_WIDTH = NSA_KV * HEAD_DIM
CMP_BLOCK = 32
CMP_STRIDE = 16
CMP_HIDDEN = 256
SEL_BLOCK = 64
N_SELECT = 16
N_FORCED = 3
WINDOW = 512
N_BUCKETS = 32
MAX_DISTANCE = 128
N_GROUPS = 4
EXPERTS_PER_GROUP = 8
N_EXPERTS = N_GROUPS * EXPERTS_PER_GROUP
EXPERT_FF = 256
PLE_DIM = 256
PAGE = 128
RMS_EPS = 1e-6
NEG = -1e30
LANES = 128
EXP_UNDERFLOW = -104.0
VMEM_LIMIT = 56 * 1024 * 1024

SEL_PAD = 7 * LANES
WIN_PAD = WINDOW


def _cparams(sem):
    return pltpu.CompilerParams(dimension_semantics=sem, vmem_limit_bytes=VMEM_LIMIT)


def _rms(x, g):
    return x * lax.rsqrt(jnp.mean(x * x, axis=-1, keepdims=True) + RMS_EPS) * g


def _dot(a, b):
    return jnp.dot(a, b, preferred_element_type=F32)


def _dot_hp(a, b):
    return jnp.dot(a, b, preferred_element_type=F32, precision=HIGHEST)


def _dot_nt(a, b):
    return lax.dot_general(a, b, (((1,), (1,)), ((), ())), preferred_element_type=F32)


def _r16(a):
    return a.astype(BF16).astype(F32)


def _t5_bucket_np(n):
    n = np.maximum(n, 0)
    max_exact = N_BUCKETS // 2
    nf = np.maximum(n, 1).astype(np.float32)
    large = max_exact + (np.log(nf / np.float32(max_exact)) / np.float32(math.log(MAX_DISTANCE / max_exact))
                         * np.float32(N_BUCKETS - max_exact)).astype(np.int32)
    large = np.minimum(large, N_BUCKETS - 1)
    return np.where(n < max_exact, n, large)


def _proj_kernel(x_ref, g_ref, w_ref, wt_ref,
                 qsb_ref, ksb_ref, vsb_ref, vsbb_ref, qn_ref,
                 kc_ref, vc_ref, ks_ref, vs_ref, kw_ref, vw_ref,
                 vsa_ref, vwa_ref, ksbt_ref, kst_ref, kwt_ref):
    xb = _rms(x_ref[...], g_ref[...]).astype(BF16)

    def cols(a, b):
        return _dot(xb, w_ref[:, a:b])

    u = cols(0, 512)
    for h in range(SB_HEADS):
        qsb_ref[h] = u[:, h * HEAD_DIM:(h + 1) * HEAD_DIM].astype(BF16)
    ksb_ref[...] = cols(512, 1024)
    v = cols(1024, 1536)
    vsb_ref[...] = v
    vsbb_ref[...] = v.astype(BF16)
    u = cols(1536, 2048)
    for h in range(NSA_HEADS):
        qn_ref[h] = u[:, h * HEAD_DIM:(h + 1) * HEAD_DIM].astype(BF16)
    for i, r in enumerate((kc_ref, vc_ref, ks_ref, vs_ref, kw_ref, vw_ref)):
        r[...] = cols(2048 + i * KV_WIDTH, 2048 + (i + 1) * KV_WIDTH)
    ones_col = lax.broadcasted_iota(jnp.int32, (xb.shape[0], LANES), 1) == HEAD_DIM
    base = 2048 + 6 * KV_WIDTH
    for kv in range(NSA_KV):
        a = cols(base + kv * LANES, base + (kv + 1) * LANES)
        vsa_ref[kv] = jnp.where(ones_col, 1.0, a).astype(BF16)
        a = cols(base + (NSA_KV + kv) * LANES, base + (NSA_KV + kv + 1) * LANES)
        vwa_ref[kv] = jnp.where(ones_col, 1.0, a).astype(BF16)
    ksbt_ref[...] = _dot_nt(wt_ref[0:512, :], xb).astype(BF16)
    kst_ref[...] = _dot_nt(wt_ref[512:640, :], xb).astype(BF16)
    kwt_ref[...] = _dot_nt(wt_ref[640:768, :], xb).astype(BF16)


def _proj(x, g, w_all, w_t, tm):
    s = x.shape[0]
    nw = w_all.shape[1]
    row = lambda n: pl.BlockSpec((tm, n), lambda i: (i, 0))
    head = pl.BlockSpec((SB_HEADS, tm, HEAD_DIM), lambda i: (0, i, 0))
    aug = pl.BlockSpec((NSA_KV, tm, LANES), lambda i: (0, i, 0))
    tr = lambda n: pl.BlockSpec((n, tm), lambda i: (0, i))
    sd = jax.ShapeDtypeStruct
    out_shape = (
        sd((SB_HEADS, s, HEAD_DIM), BF16), sd((s, SB_WIDTH), F32), sd((s, SB_WIDTH), F32),
        sd((s, SB_WIDTH), BF16), sd((NSA_HEADS, s, HEAD_DIM), BF16),
        *(sd((s, KV_WIDTH), F32) for _ in range(6)),
        sd((NSA_KV, s, LANES), BF16), sd((NSA_KV, s, LANES), BF16),
        sd((SB_WIDTH, s), BF16), sd((KV_WIDTH, s), BF16), sd((KV_WIDTH, s), BF16))
    out_specs = (head, row(SB_WIDTH), row(SB_WIDTH), row(SB_WIDTH), head,
                 *(row(KV_WIDTH) for _ in range(6)), aug, aug,
                 tr(SB_WIDTH), tr(KV_WIDTH), tr(KV_WIDTH))
    return pl.pallas_call(
        _proj_kernel,
        grid=(s // tm,),
        in_specs=[row(D_MODEL), pl.BlockSpec((1, D_MODEL), lambda i: (0, 0)),
                  pl.BlockSpec((D_MODEL, nw), lambda i: (0, 0)),
                  pl.BlockSpec((w_t.shape[0], D_MODEL), lambda i: (0, 0))],
        out_specs=out_specs,
        out_shape=out_shape,
        compiler_params=_cparams(("parallel",)),
        name="proj_prompt",
    )(x, g, w_all, w_t)


def _proj_s_kernel(x_ref, g_ref, w_ref, o_ref):
    o_ref[...] = _dot(_rms(x_ref[...], g_ref[...]).astype(BF16), w_ref[...])


def _proj_s(x, g, w):
    b, n = x.shape[0], w.shape[1]
    return pl.pallas_call(
        _proj_s_kernel,
        out_shape=jax.ShapeDtypeStruct((b, n), F32),
        compiler_params=pltpu.CompilerParams(vmem_limit_bytes=VMEM_LIMIT),
        name="proj_sample",
    )(x, g, w)


def _sb_kernel(q_ref, kt_ref, v_ref, tri_ref, o_ref, *, tq):
    qi = pl.program_id(1)
    tri = tri_ref[...]
    row = lax.broadcasted_iota(jnp.int32, (tq, tq), 0)
    col = lax.broadcasted_iota(jnp.int32, (tq, tq), 1)
    below = col < row
    res = []
    for hh in range(2):
        q = q_ref[hh]

        def tile(j, run, diagonal, hh=hh, q=q):
            ks = pl.multiple_of(j * tq, tq)
            kt = kt_ref[hh * HEAD_DIM:(hh + 1) * HEAD_DIM, pl.ds(ks, tq)]
            z = _dot(q, kt)
            t = jnp.log(1.0 + jnp.exp(-jnp.abs(z)))
            log_keep = -(jnp.maximum(z, 0.0) + t)
            if diagonal:
                log_keep = jnp.where(below, log_keep, 0.0)
            after = _dot(log_keep.astype(BF16), tri)
            a = jnp.exp(jnp.minimum(z, 0.0) - t + after + run)
            if diagonal:
                a = jnp.where(below, a, 0.0)
            pv = _dot(a.astype(BF16), v_ref[pl.ds(ks, tq), :])
            return pv, run + after[:, 0:1] + log_keep[:, 0:1]

        pv0, run0 = tile(qi, jnp.zeros((tq, 1), F32), True)

        def cond(c):
            return jnp.logical_and(c[0] <= qi, c[3])

        def body(c):
            jj, acc, run, _ = c
            pv, run = tile(qi - jj, run, False)
            return jj + 1, acc + pv, run, jnp.max(run) > EXP_UNDERFLOW

        c = lax.while_loop(cond, body, (jnp.int32(1), pv0, run0, jnp.max(run0) > EXP_UNDERFLOW))
        res.append(c[1])
    lane = lax.broadcasted_iota(jnp.int32, (tq, LANES), 1)
    o_ref[...] = jnp.where(lane < HEAD_DIM, res[0], res[1])


def _sb_prompt(q_h, k_t, v_b, tq):
    s = k_t.shape[1]
    tri = jnp.asarray(np.tril(np.ones((tq, tq), np.float32), -1), BF16)
    return pl.pallas_call(
        functools.partial(_sb_kernel, tq=tq),
        grid=(SB_HEADS // 2, s // tq),
        in_specs=[pl.BlockSpec((2, tq, HEAD_DIM), lambda p, i: (p, i, 0)),
                  pl.BlockSpec((LANES, s), lambda p, i: (p, 0)),
                  pl.BlockSpec((s, LANES), lambda p, i: (0, p)),
                  pl.BlockSpec((tq, tq), lambda p, i: (0, 0))],
        out_specs=pl.BlockSpec((tq, LANES), lambda p, i: (i, p)),
        out_shape=jax.ShapeDtypeStruct((s, SB_WIDTH), F32),
        compiler_params=_cparams(("parallel", "parallel")),
        name="sb_prompt",
    )(q_h, k_t, v_b, tri)


def _compress_rows(x, pe_a, pe_b, w_a, w_b, b1):
    m = x.shape[0]
    a = _dot((x + pe_a).astype(BF16), w_a)
    b = _dot((x + pe_b).astype(BF16), w_b)
    b = pltpu.roll(b, m - 1, 0)
    return jax.nn.gelu(a + b + b1)


def _compress_kernel(x_ref, pea_ref, peb_ref, wa_ref, wb_ref, b1_ref, w2_ref, w2t_ref, w2a_ref,
                     cmp_t_ref, cmp_a_ref):
    hid = _compress_rows(x_ref[...], pea_ref[...], peb_ref[...], wa_ref[...], wb_ref[...],
                         b1_ref[...]).astype(BF16)
    cmp_t_ref[...] = _dot_nt(w2t_ref[...], hid).astype(BF16)
    for kv in range(NSA_KV):
        cmp_a_ref[kv] = _dot(hid, w2a_ref[kv]).astype(BF16)


def _compress_prompt(x2, cw):
    m = x2.shape[1]
    lead = lambda *shape: pl.BlockSpec((None,) + shape, lambda i: (i,) + (0,) * len(shape))
    return pl.pallas_call(
        _compress_kernel,
        grid=(2,),
        in_specs=[lead(m, 16 * KV_WIDTH), lead(1, 16 * KV_WIDTH), lead(1, 16 * KV_WIDTH),
                  lead(16 * KV_WIDTH, 2 * CMP_HIDDEN), lead(16 * KV_WIDTH, 2 * CMP_HIDDEN),
                  lead(1, 2 * CMP_HIDDEN), lead(2 * CMP_HIDDEN, KV_WIDTH),
                  lead(KV_WIDTH, 2 * CMP_HIDDEN), lead(NSA_KV, 2 * CMP_HIDDEN, LANES)],
        out_specs=(lead(KV_WIDTH, m), lead(NSA_KV, m, LANES)),
        out_shape=(jax.ShapeDtypeStruct((2, KV_WIDTH, m), BF16),
                   jax.ShapeDtypeStruct((2, NSA_KV, m, LANES), BF16)),
        compiler_params=_cparams(("parallel",)),
        name="compress_prompt",
    )(x2, cw["pe_a"], cw["pe_b"], cw["w_a"], cw["w_b"], cw["b1"], cw["w2"], cw["w2t"], cw["w2a"])


def _compress_weights(pe, w1, b1, w2):
    half = CMP_BLOCK // 2
    eye = jnp.eye(NSA_KV, dtype=F32)
    w1r = w1.reshape(2, half, HEAD_DIM, CMP_HIDDEN)

    def big(wh):
        return jnp.einsum("rdf,pk->rpdkf", wh, eye).reshape(half * KV_WIDTH, NSA_KV * CMP_HIDDEN)

    def perow(p):
        return jnp.broadcast_to(p[:, None, :], (half, NSA_KV, HEAD_DIM)).reshape(1, half * KV_WIDTH)

    w2bd = jnp.einsum("fd,pk->pfkd", w2, eye).reshape(NSA_KV * CMP_HIDDEN, KV_WIDTH)
    w2a = jnp.stack([jnp.concatenate(
        [w2bd[:, kv * HEAD_DIM:(kv + 1) * HEAD_DIM], jnp.zeros((NSA_KV * CMP_HIDDEN, LANES - HEAD_DIM), F32)],
        axis=1) for kv in range(NSA_KV)])
    return dict(pe_a=perow(pe[:half]), pe_b=perow(pe[half:]),
                w_a=big(w1r[0]).astype(BF16), w_b=big(w1r[1]).astype(BF16),
                b1=jnp.tile(b1, NSA_KV)[None], w2=w2bd.astype(BF16), w2t=w2bd.T.astype(BF16),
                w2a=w2a.astype(BF16))


def _top_select(vals, idx, forced, axis, rounds):
    sel = forced
    for _ in range(rounds):
        mx = jnp.max(vals, axis=axis, keepdims=True)
        first = jnp.min(jnp.where(vals == mx, idx, jnp.int32(1 << 30)), axis=axis, keepdims=True)
        hit = jnp.logical_and(idx == first, mx > 0.5 * NEG)
        sel = jnp.logical_or(sel, hit)
        vals = jnp.where(hit, NEG, vals)
    return sel


def _pack_heads(o, tq):
    lane = lax.broadcasted_iota(jnp.int32, (tq, LANES), 1)
    o4 = o.reshape(NSA_GROUP, tq, LANES)
    halves = []
    for a in range(NSA_GROUP // 2):
        halves.append(jnp.where(lane < HEAD_DIM, o4[2 * a], pltpu.roll(o4[2 * a + 1], HEAD_DIM, 1)))
    return halves


def _nsa_cmp_kernel(q_ref, ltab_ref, kt_ref, v_ref, ov_ref, o_ref, sel_ref, *, tq, n_blk):
    qi = pl.program_id(0)
    rows = NSA_GROUP * tq
    n_c = kt_ref.shape[2]
    r = lax.broadcasted_iota(jnp.int32, (HEAD_DIM, n_c), 0)
    c = lax.broadcasted_iota(jnp.int32, (HEAD_DIM, n_c), 1)
    place = jnp.where(jnp.logical_and(r < 32, c == (tq // CMP_STRIDE) * qi + (r & 15) - 9), 1.0, 0.0).astype(BF16)
    qpos = qi * tq + (lax.broadcasted_iota(jnp.int32, (rows, 1), 0) & (tq - 1))
    c_max = (qpos - (CMP_BLOCK - 1)) >> 4
    mask = lax.broadcasted_iota(jnp.int32, (rows, n_c), 1) <= c_max
    imp = []
    for kv in range(NSA_KV):
        q4 = q_ref[kv * NSA_GROUP:(kv + 1) * NSA_GROUP].reshape(rows, HEAD_DIM)
        s = _dot(q4, kt_ref[kv]) + _dot(ltab_ref[kv], place)
        s = jnp.where(mask, s, NEG)
        m = jnp.max(s, axis=-1, keepdims=True)
        e = jnp.where(mask, jnp.exp(s - m), 0.0)
        p = e / jnp.maximum(jnp.sum(e, axis=-1, keepdims=True), 1e-30)
        h0, h1 = _pack_heads(_dot(p.astype(BF16), v_ref[kv]), tq)
        o_ref[:, (2 * kv) * LANES:(2 * kv + 1) * LANES] = h0
        o_ref[:, (2 * kv + 1) * LANES:(2 * kv + 2) * LANES] = h1
        p4 = p.reshape(NSA_GROUP, tq, n_c)
        imp.append(_dot((p4[0] + p4[1] + p4[2] + p4[3]).astype(BF16), ov_ref[...]))
    imp = jnp.stack(imp)
    j = lax.broadcasted_iota(jnp.int32, (NSA_KV, tq, n_blk), 2)
    q_blk = (qi * tq + lax.broadcasted_iota(jnp.int32, (NSA_KV, tq, 1), 1)) >> 6
    forced = jnp.logical_or(j == 0, jnp.logical_and(j >= q_blk - 1, j <= q_blk))
    cand = jnp.logical_and(j >= 1, j <= q_blk - 2)
    sel = _top_select(jnp.where(cand, imp, NEG), j, forced, 2, N_SELECT - N_FORCED)
    sel_ref[...] = jnp.where(sel, 1.0, 0.0).astype(BF16)


def _nsa_cmp_prompt(qn_h, ltab, cmp_kt, cmp_va, overlap, tq):
    s = qn_h.shape[1]
    n_c = cmp_kt.shape[2]
    n_blk = overlap.shape[1]
    full = lambda a: pl.BlockSpec(a.shape, lambda i: (0,) * a.ndim)
    return pl.pallas_call(
        functools.partial(_nsa_cmp_kernel, tq=tq, n_blk=n_blk),
        grid=(s // tq,),
        in_specs=[pl.BlockSpec((NSA_HEADS, tq, HEAD_DIM), lambda i: (0, i, 0)),
                  full(ltab), full(cmp_kt), full(cmp_va), full(overlap)],
        out_specs=(pl.BlockSpec((tq, NSA_WIDTH), lambda i: (i, 0)),
                   pl.BlockSpec((NSA_KV, tq, n_blk), lambda i: (0, i, 0))),
        out_shape=(jax.ShapeDtypeStruct((s, NSA_WIDTH), F32),
                   jax.ShapeDtypeStruct((NSA_KV, s, n_blk), BF16)),
        compiler_params=_cparams(("parallel",)),
        name="nsa_cmp_prompt",
    )(qn_h, ltab, cmp_kt, cmp_va, overlap)


def _nsa_sel_kernel(q_ref, kt_ref, v_ref, sel_ref, pn_ref, o_ref, *, tq, n_blk):
    qi = pl.program_id(1)
    rows = NSA_GROUP * tq
    tk_far = SEL_PAD + LANES
    q4 = q_ref[...].reshape(rows, HEAD_DIM)
    selm = sel_ref[...]

    def tile(start, tk, bias, m, acc):
        ps = pl.multiple_of(start + SEL_PAD, LANES)
        jb = lax.broadcasted_iota(jnp.int32, (n_blk, tk), 0)
        lk = lax.broadcasted_iota(jnp.int32, (n_blk, tk), 1)
        expand = jnp.where(jb == ((start + lk) >> 6), 1.0, 0.0).astype(BF16)
        drop = (_dot(selm, expand) - 1.0) * (-NEG)
        s = _dot(q4, kt_ref[:, pl.ds(ps, tk)])
        if bias is not None:
            s = s + bias
        s3 = s.reshape(NSA_GROUP, tq, tk) + drop[None]
        m_new = jnp.maximum(m, jnp.max(s3, axis=-1, keepdims=True))
        p = jnp.exp(s3 - m_new).reshape(rows, tk)
        alpha = jnp.exp(m - m_new).reshape(rows, 1)
        acc = alpha * acc + _dot(p.astype(BF16), v_ref[pl.ds(ps, tk), :])
        return m_new, acc

    near = (qi - 1) * tq
    m0 = jnp.full((NSA_GROUP, tq, 1), NEG, F32)
    m, acc = tile(near, 2 * tq, pn_ref[...], m0, jnp.zeros((rows, LANES), F32))

    def body(t, c):
        return tile(near - tk_far * (t + 1), tk_far, None, *c)

    m, acc = lax.fori_loop(0, (near + tk_far - 1) // tk_far, body, (m, acc))
    o = acc / jnp.maximum(acc[:, HEAD_DIM:HEAD_DIM + 1], 1e-30)
    h0, h1 = _pack_heads(o, tq)
    o_ref[:, 0:LANES] = h0
    o_ref[:, LANES:2 * LANES] = h1


def _nsa_sel_prompt(qn_h, ks_t, vs_a, sel, p_near, tq):
    s = qn_h.shape[1]
    sp = ks_t.shape[2]
    n_blk = sel.shape[2]
    return pl.pallas_call(
        functools.partial(_nsa_sel_kernel, tq=tq, n_blk=n_blk),
        grid=(NSA_KV, s // tq),
        in_specs=[pl.BlockSpec((NSA_GROUP, tq, HEAD_DIM), lambda k, i: (k, i, 0)),
                  pl.BlockSpec((None, HEAD_DIM, sp), lambda k, i: (k, 0, 0)),
                  pl.BlockSpec((None, sp, LANES), lambda k, i: (k, 0, 0)),
                  pl.BlockSpec((None, tq, n_blk), lambda k, i: (k, i, 0)),
                  pl.BlockSpec((None, NSA_GROUP * tq, 2 * tq), lambda k, i: (k, 0, 0))],
        out_specs=pl.BlockSpec((tq, 2 * LANES), lambda k, i: (i, k)),
        out_shape=jax.ShapeDtypeStruct((s, NSA_WIDTH), F32),
        compiler_params=_cparams(("parallel", "parallel")),
        name="nsa_sel_prompt",
    )(qn_h, ks_t, vs_a, sel, p_near)


def _nsa_win_kernel(q_ref, kt_ref, v_ref, pw_ref, o_ref, *, tq):
    qi = pl.program_id(1)
    rows = NSA_GROUP * tq
    span = WINDOW + tq
    q4 = q_ref[...].reshape(rows, HEAD_DIM)
    ps = pl.multiple_of(qi * tq, LANES)
    s = _dot(q4, kt_ref[:, pl.ds(ps, span)]) + pw_ref[...]
    real = lax.broadcasted_iota(jnp.int32, (rows, span), 1) >= WIN_PAD - qi * tq
    s = jnp.where(real, s, NEG)
    m = jnp.max(s, axis=-1, keepdims=True)
    e = jnp.where(real, jnp.exp(s - m), 0.0)
    acc = _dot(e.astype(BF16), v_ref[pl.ds(ps, span), :])
    o = acc / jnp.maximum(acc[:, HEAD_DIM:HEAD_DIM + 1], 1e-30)
    h0, h1 = _pack_heads(o, tq)
    o_ref[:, 0:LANES] = h0
    o_ref[:, LANES:2 * LANES] = h1


def _nsa_win_prompt(qn_h, kw_t, vw_a, p_win, tq):
    s = qn_h.shape[1]
    sp = kw_t.shape[2]
    return pl.pallas_call(
        functools.partial(_nsa_win_kernel, tq=tq),
        grid=(NSA_KV, s // tq),
        in_specs=[pl.BlockSpec((NSA_GROUP, tq, HEAD_DIM), lambda k, i: (k, i, 0)),
                  pl.BlockSpec((None, HEAD_DIM, sp), lambda k, i: (k, 0, 0)),
                  pl.BlockSpec((None, sp, LANES), lambda k, i: (k, 0, 0)),
                  pl.BlockSpec((None, NSA_GROUP * tq, WINDOW + tq), lambda k, i: (k, 0, 0))],
        out_specs=pl.BlockSpec((tq, 2 * LANES), lambda k, i: (i, k)),
        out_shape=jax.ShapeDtypeStruct((s, NSA_WIDTH), F32),
        compiler_params=_cparams(("parallel", "parallel")),
        name="nsa_win_prompt",
    )(qn_h, kw_t, vw_a, p_win)


def _route(logits):
    lane = lax.broadcasted_iota(jnp.int32, logits.shape, 1)
    big = jnp.int32(1 << 30)
    is_g = jnp.logical_and(lane >= N_EXPERTS, lane < N_EXPERTS + N_GROUPS)
    gl = jnp.where(is_g, logits, NEG)
    gmax = jnp.max(gl, axis=-1, keepdims=True)
    gidx = jnp.min(jnp.where(gl == gmax, lane - N_EXPERTS, big), axis=-1, keepdims=True)
    g_w = 1.0 / jnp.sum(jnp.where(is_g, jnp.exp(gl - gmax), 0.0), axis=-1, keepdims=True)
    in_g = jnp.logical_and(lane < N_EXPERTS, (lane >> 3) == gidx)
    el = jnp.where(in_g, logits, NEG)
    emax = jnp.max(el, axis=-1, keepdims=True)
    ee = jnp.where(in_g, jnp.exp(el - emax), 0.0)
    prob = jnp.where(in_g, ee / jnp.sum(ee, axis=-1, keepdims=True), -1.0)
    p1 = jnp.max(prob, axis=-1, keepdims=True)
    i1 = jnp.min(jnp.where(prob == p1, lane, big), axis=-1, keepdims=True)
    prob2 = jnp.where(lane == i1, -1.0, prob)
    p2 = jnp.max(prob2, axis=-1, keepdims=True)
    i2 = jnp.min(jnp.where(prob2 == p2, lane, big), axis=-1, keepdims=True)
    tot = p1 + p2
    return jnp.where(lane == i1, g_w * (p1 / tot), jnp.where(lane == i2, g_w * (p2 / tot), 0.0))


def _merge_kernel(x_ref, osb_ref, oc_ref, os_ref, ow_ref, gmix_ref, wg_ref, eg_ref, wbs_ref, wbn_ref,
                  wo_ref, gffn_ref, wr_ref, h_ref, hn_ref, comb_ref):
    mm = lambda a, w_ref: _dot(a.astype(BF16), w_ref[...])
    x = x_ref[...]
    u = mm(_rms(x, gmix_ref[...]), wg_ref)
    g = jax.nn.sigmoid(u[:, 0:LANES])
    g_hi = _r16(g)
    g_mid = _r16(g - g_hi)
    g_br = mm(g_hi, eg_ref) + mm(g_mid, eg_ref) + mm(g - g_hi - g_mid, eg_ref)
    o_nsa = (g_br[:, 0:NSA_WIDTH] * oc_ref[...] + g_br[:, NSA_WIDTH:2 * NSA_WIDTH] * os_ref[...]
             + g_br[:, 2 * NSA_WIDTH:3 * NSA_WIDTH] * ow_ref[...])
    merged = (jax.nn.sigmoid(u[:, LANES:LANES + D_MODEL]) * mm(osb_ref[...], wbs_ref)
              + jax.nn.sigmoid(u[:, LANES + D_MODEL:LANES + 2 * D_MODEL]) * mm(o_nsa, wbn_ref))
    h = x + mm(merged, wo_ref)
    hn = _rms(h, gffn_ref[...])
    h_ref[...] = h
    hn_ref[...] = hn.astype(BF16)
    comb_ref[...] = _route(mm(hn, wr_ref))


def _merge(x, o_sb, o_c, o_s, o_w, mw, tm):
    n = x.shape[0]
    row = lambda c: pl.BlockSpec((tm, c), lambda i: (i, 0))
    full = lambda a: pl.BlockSpec(a.shape, lambda i: (0,) * a.ndim)
    ws = (mw["g_mix"], mw["w_gate"], mw["e_g"], mw["w_b_sb"], mw["w_b_nsa"], mw["w_o"], mw["g_ffn"], mw["w_r"])
    return pl.pallas_call(
        _merge_kernel,
        grid=(n // tm,),
        in_specs=[row(D_MODEL), row(SB_WIDTH), row(NSA_WIDTH), row(NSA_WIDTH), row(NSA_WIDTH)]
                 + [full(a) for a in ws],
        out_specs=(row(D_MODEL), row(D_MODEL), row(LANES)),
        out_shape=(jax.ShapeDtypeStruct((n, D_MODEL), F32), jax.ShapeDtypeStruct((n, D_MODEL), BF16),
                   jax.ShapeDtypeStruct((n, LANES), F32)),
        compiler_params=_cparams(("parallel",)),
        name="merge",
    )(x, o_sb, o_c, o_s, o_w, *ws)


def _moe_kernel(h_ref, hn_ref, comb_ref, pe_ref, weg_ref, weu_ref, wed_ref, gple_ref, wpg_ref, wpe_ref,
                gfin_ref, y_ref, acc_ref):
    e = pl.program_id(1)

    @pl.when(e == 0)
    def _():
        acc_ref[...] = jnp.zeros_like(acc_ref)

    hn = hn_ref[...]
    lane = lax.broadcasted_iota(jnp.int32, comb_ref.shape, 1)
    ce = jnp.sum(jnp.where(lane == e, comb_ref[...], 0.0), axis=-1, keepdims=True)
    hid = jax.nn.silu(_dot(hn, weg_ref[...])) * _dot(hn, weu_ref[...]) * ce
    acc_ref[...] += _dot(hid.astype(BF16), wed_ref[...])

    @pl.when(e == N_EXPERTS - 1)
    def _():
        h = h_ref[...] + acc_ref[...]
        gate = jax.nn.sigmoid(_dot(_rms(h, gple_ref[...]).astype(BF16), wpg_ref[...]))
        h = h + gate * _dot(pe_ref[...].astype(BF16), wpe_ref[...])
        y_ref[...] = _rms(h, gfin_ref[...])


def _moe(h, hn, comb, p_emb, ew, tm):
    n = h.shape[0]
    row = lambda c: pl.BlockSpec((tm, c), lambda i, e: (i, 0))
    full = lambda a: pl.BlockSpec(a.shape, lambda i, e: (0,) * a.ndim)
    exp = lambda a, b: pl.BlockSpec((None, a, b), lambda i, e: (e, 0, 0))
    return pl.pallas_call(
        _moe_kernel,
        grid=(n // tm, N_EXPERTS),
        in_specs=[row(D_MODEL), row(D_MODEL), row(LANES), row(PLE_DIM),
                  exp(D_MODEL, EXPERT_FF), exp(D_MODEL, EXPERT_FF), exp(EXPERT_FF, D_MODEL),
                  full(ew["g_ple"]), full(ew["w_pg"]), full(ew["w_pe"]), full(ew["g_fin"])],
        out_specs=row(D_MODEL),
        out_shape=jax.ShapeDtypeStruct((n, D_MODEL), F32),
        scratch_shapes=[pltpu.VMEM((tm, D_MODEL), F32)],
        compiler_params=_cparams(("parallel", "arbitrary")),
        name="moe_ple",
    )(h, hn, comb, p_emb, ew["w_eg"], ew["w_eu"], ew["w_ed"], ew["g_ple"], ew["w_pg"], ew["w_pe"], ew["g_fin"])


PAGES_PER_STEP = 8


def _decode_kernel(pt_ref, qcol_ref, qrow_ref, qz_ref, sbk_hbm, sbv_hbm, *refs, n_pages, n_c, n_sel):
    pps = PAGES_PER_STEP
    pages = [refs[a * pps:(a + 1) * pps] for a in range(4)]
    (tri_ref, pe_ref, w1a_ref, w1b_ref, b1_ref, w2_ref, bc_ref, bs_ref, bw_ref, bn_ref, ov_ref, ex_ref,
     nks_ref, nvs_ref, nkw_ref, nvw_ref, wk_ref, wv_ref,
     osb_ref, oc_ref, os_ref, ow_ref,
     x2_scr, stage_scr, s_scr, vt_scr, acc_scr, run_scr, sbk_buf, sbv_buf, sb_sem) = refs[4 * pps:]
    p = pl.program_id(1)
    seq_pages = pl.program_id(0) * n_pages
    m_rows = n_pages * (PAGE // CMP_STRIDE)
    per_page = PAGE // CMP_STRIDE
    row8 = lax.broadcasted_iota(jnp.int32, (8, HEAD_DIM), 0)
    first_kv = row8 < NSA_GROUP

    @pl.when(p == 0)
    def _():
        acc_scr[...] = jnp.zeros_like(acc_scr)
        run_scr[...] = jnp.zeros_like(run_scr)

    def nsa_scores(kt0, kt1):
        return (_dot(qz_ref[0], kt0.astype(BF16)) + _dot(qz_ref[1], kt1.astype(BF16)))[0:8]

    def pad16(a):
        return jnp.concatenate([a, jnp.zeros_like(a)], axis=0).astype(BF16)

    def by_kv(a0, a1):
        return jnp.where(first_kv, a0[0:8], a1[0:8])

    alive = jnp.max(run_scr[...]) > EXP_UNDERFLOW

    def sb_copies():
        cps = []
        for u in range(pps):
            src = pt_ref[seq_pages + n_pages - 1 - (p * pps + u)]
            cps.append(pltpu.make_async_copy(sbk_hbm.at[src], sbk_buf.at[u], sb_sem.at[0, u]))
            cps.append(pltpu.make_async_copy(sbv_hbm.at[src], sbv_buf.at[u], sb_sem.at[1, u]))
        return cps

    @pl.when(alive)
    def _():
        for cp in sb_copies():
            cp.start()

    for u in range(pps):
        page = n_pages - 1 - (p * pps + u)
        cck_ref, ccv_ref, slk_ref, slv_ref = (pages[a][u] for a in range(4))
        crow = pl.multiple_of(page * per_page, per_page)
        for i, ref in enumerate((cck_ref, ccv_ref)):
            for kv in range(NSA_KV):
                stage = stage_scr.at[(u * 2 + i) * NSA_KV + kv]
                stage[...] = ref[kv].T
                for r2 in range(CMP_STRIDE // 2):
                    pair = [stage[pl.ds(r, per_page, stride=CMP_STRIDE), :] for r in (2 * r2, 2 * r2 + 1)]
                    x2_scr[i, pl.ds(kv * m_rows + crow, per_page), r2 * LANES:(r2 + 1) * LANES] = (
                        jnp.concatenate(pair, axis=1))

        col = pl.ds(pl.multiple_of(page * PAGE, PAGE), PAGE)
        s_scr[:, col] = nsa_scores(slk_ref[0], slk_ref[1])
        for kv in range(NSA_KV):
            vt_scr[kv, :, col] = slv_ref[kv].astype(BF16)

    @pl.when(alive)
    def _():
        for cp in sb_copies():
            cp.wait()

        def sb_pages(us):
            run = run_scr[...]
            o = jnp.zeros((8, HEAD_DIM), F32)
            for u in us:
                sbk_ref, sbv_ref = sbk_buf.at[u], sbv_buf.at[u]
                z = jnp.concatenate(
                    [jnp.sum(_r16(sbk_ref[h]) * _r16(qcol_ref[h]), axis=0, keepdims=True)
                     for h in range(SB_HEADS)], axis=0)
                t = jnp.log(1.0 + jnp.exp(-jnp.abs(z)))
                log_keep = -(jnp.maximum(z, 0.0) + t)
                after = _dot_hp(log_keep, tri_ref[...])
                a = _r16(jnp.exp(jnp.minimum(z, 0.0) - t + after + run))
                for h in range(SB_HEADS):
                    oh = lax.dot_general(a, _r16(sbv_ref[h]), (((1,), (1,)), ((), ())),
                                         preferred_element_type=F32, precision=HIGHEST)
                    o = o + jnp.where(row8 == h, oh, 0.0)
                run = run + jnp.sum(log_keep, axis=1, keepdims=True)
            acc_scr[...] += o
            run_scr[...] = run

        sb_pages((0, 1))
        for first in range(2, pps, 2):
            @pl.when(jnp.max(run_scr[...]) > EXP_UNDERFLOW)
            def _(first=first):
                sb_pages((first, first + 1))

    @pl.when(p == n_pages // pps - 1)
    def _():
        osb_ref[...] = acc_scr[...]
        cmp = []
        for i in range(2):
            x2 = x2_scr[i]
            a = _dot((x2 + pe_ref[i, 0:1, :]).astype(BF16), w1a_ref[i])
            b = _dot((x2 + pe_ref[i, 1:2, :]).astype(BF16), w1b_ref[i])
            b = pltpu.roll(b, NSA_KV * m_rows - 1, 0)
            cmp.append(_dot(jax.nn.gelu(a + b + b1_ref[i]).astype(BF16), w2_ref[i]))
        ck, cv = cmp
        s = (_dot_nt(qz_ref[0], ck[0:m_rows].astype(BF16))
             + _dot_nt(qz_ref[1], ck[m_rows:].astype(BF16)))[0:8] + bc_ref[...]
        valid = lax.broadcasted_iota(jnp.int32, s.shape, 1) < n_c
        s = jnp.where(valid, s, NEG)
        e = jnp.where(valid, jnp.exp(s - jnp.max(s, axis=1, keepdims=True)), 0.0)
        pc = e / jnp.maximum(jnp.sum(e, axis=1, keepdims=True), 1e-30)
        pc16 = pad16(pc)
        oc_ref[...] = by_kv(_dot(pc16, cv[0:m_rows].astype(BF16)), _dot(pc16, cv[m_rows:].astype(BF16)))
        g_first = lax.broadcasted_iota(jnp.int32, pc.shape, 0) < NSA_GROUP
        pg = jnp.where(g_first, jnp.sum(jnp.where(g_first, pc, 0.0), axis=0, keepdims=True),
                       jnp.sum(jnp.where(g_first, 0.0, pc), axis=0, keepdims=True))
        imp = _dot_hp(_r16(pg), ov_ref[...])
        j = lax.broadcasted_iota(jnp.int32, imp.shape, 1)
        forced = jnp.logical_or(j == 0, jnp.logical_and(j >= n_sel - 2, j < n_sel))
        cand = jnp.logical_and(j >= 1, j < n_sel - 2)
        sel = _top_select(jnp.where(cand, imp, NEG), j, forced, 1, N_SELECT - N_FORCED)

        def attend(s_past, vt0, vt1, k_new_ref, v_new_ref):
            kn, vn = _r16(k_new_ref[...]), _r16(v_new_ref[...])
            s_new = jnp.sum(_r16(qrow_ref[...]) * jnp.where(first_kv, kn[0:1], kn[1:2]), axis=1, keepdims=True)
            s_new = s_new + bn_ref[...]
            m = jnp.maximum(jnp.max(s_past, axis=1, keepdims=True), s_new)
            e_past = jnp.where(s_past > 0.5 * NEG, jnp.exp(s_past - m), 0.0)
            e_new = jnp.exp(s_new - m)
            l = jnp.maximum(jnp.sum(e_past, axis=1, keepdims=True) + e_new, 1e-30)
            p16 = pad16(e_past / l)
            o = by_kv(_dot_nt(p16, vt0), _dot_nt(p16, vt1))
            return o + _r16(e_new / l) * jnp.where(first_kv, vn[0:1], vn[1:2])

        keep = _dot(pad16(jnp.where(sel, 1.0, 0.0)), ex_ref[...])[0:8] > 0.5
        s_all = jnp.where(keep, s_scr[...] + bs_ref[...], NEG)
        os_ref[...] = attend(s_all, vt_scr[0], vt_scr[1], nks_ref, nvs_ref)
        in_win = lax.broadcasted_iota(jnp.int32, (8, wk_ref.shape[2]), 1) >= 1
        s_w = jnp.where(in_win, nsa_scores(wk_ref[0], wk_ref[1]) + bw_ref[...], NEG)
        ow_ref[...] = attend(s_w, wv_ref[0].astype(BF16), wv_ref[1].astype(BF16), nkw_ref, nvw_ref)


def _decode(pt, q_col, q_row, q_z, caches, new_rows, win_k, win_v, consts, n_c, n_sel):
    b = q_row.shape[0]
    n_pages = pt.shape[0] // b
    pps = PAGES_PER_STEP
    assert n_pages % pps == 0
    m_rows = n_pages * (PAGE // CMP_STRIDE)

    def page_map(u):
        return lambda bb, p, pt_ref: (pt_ref[bb * n_pages + n_pages - 1 - (p * pps + u)], 0, 0, 0)

    seq = lambda a: pl.BlockSpec((None,) + a.shape[1:], lambda bb, p, pt_ref: (bb,) + (0,) * (a.ndim - 1))
    full = lambda a: pl.BlockSpec(a.shape, lambda bb, p, pt_ref: (0,) * a.ndim)
    sb_caches, caches = caches[:2], caches[2:]
    in_hbm = pl.BlockSpec(memory_space=pl.ANY)
    paged = [pl.BlockSpec((None,) + a.shape[1:], page_map(u)) for a in caches for u in range(pps)]
    paged_args = [a for a in caches for _ in range(pps)]
    sb_page = (pps,) + sb_caches[0].shape[1:]
    cs = [consts[k] for k in ("tri", "pe", "w1a", "w1b", "b1", "w2", "b_cmp", "b_sel", "b_win", "b_new",
                              "ov", "expand")]
    per_seq = [*new_rows, win_k, win_v]
    out = jax.ShapeDtypeStruct((b, 8, HEAD_DIM), F32)
    grid_spec = pltpu.PrefetchScalarGridSpec(
        num_scalar_prefetch=1,
        grid=(b, n_pages // pps),
        in_specs=[seq(q_col), seq(q_row), seq(q_z), in_hbm, in_hbm] + paged + [full(a) for a in cs]
                 + [seq(a) for a in per_seq],
        out_specs=(pl.BlockSpec((None, 8, HEAD_DIM), lambda bb, p, pt_ref: (bb, 0, 0)),) * 4,
        scratch_shapes=[pltpu.VMEM((2, NSA_KV * m_rows, CMP_STRIDE * HEAD_DIM), F32),
                        pltpu.VMEM((pps * 2 * NSA_KV, PAGE, HEAD_DIM), F32),
                        pltpu.VMEM((8, n_pages * PAGE), F32),
                        pltpu.VMEM((NSA_KV, HEAD_DIM, n_pages * PAGE), BF16),
                        pltpu.VMEM((8, HEAD_DIM), F32), pltpu.VMEM((8, 1), F32),
                        pltpu.VMEM(sb_page, F32), pltpu.VMEM(sb_page, F32), pltpu.SemaphoreType.DMA((2, pps))])
    return pl.pallas_call(
        functools.partial(_decode_kernel, n_pages=n_pages, n_c=n_c, n_sel=n_sel),
        grid_spec=grid_spec,
        out_shape=(out,) * 4,
        compiler_params=_cparams(("parallel", "arbitrary")),
        name="decode_mix",
    )(pt, q_col, q_row, q_z, *sb_caches, *paged_args, *cs, *per_seq)


def _onehot_rows(delta, dist):
    oh = (np.arange(MAX_DISTANCE)[:, None] == np.asarray(dist)[None, :]).astype(np.float32)
    return jnp.dot(delta, jnp.asarray(oh), precision=HIGHEST)


def _bias_delta(table):
    bucket = _t5_bucket_np(np.arange(MAX_DISTANCE))
    oh = (np.arange(N_BUCKETS)[:, None] == bucket[None, :]).astype(np.float32)
    return jnp.dot((table - table[N_BUCKETS - 1][None]).T, jnp.asarray(oh), precision=HIGHEST)


def _toeplitz(f, n, w):
    lf = n + w - 1
    x = jnp.broadcast_to(f[:, None, :], (f.shape[0], n, lf))
    x = jnp.pad(x, ((0, 0), (0, 0), (0, 1))).reshape(f.shape[0], n * (lf + 1))[:, :n * lf]
    return x.reshape(f.shape[0], n, lf)[:, :, n - 1:n - 1 + w]


def _pad_lanes(a, n=LANES):
    return jnp.concatenate([a, jnp.zeros(a.shape[:-1] + (n - a.shape[-1],), a.dtype)], axis=-1)


def kernel(x_prompt, x_sample, cache_sb_k, cache_sb_v, cache_cmp_k, cache_cmp_v, cache_sel_k, cache_sel_v, state_win_k, state_win_v, page_table, p_prompt, p_sample, norm_mix, w_in, cmp_pe_k, cmp_w1_k, cmp_b1_k, cmp_w2_k, cmp_pe_v, cmp_w1_v, cmp_b1_v, cmp_w2_v, rel_bias_table, w_branch_sb, w_branch_nsa, w_out, norm_ffn, w_router_group, w_router_expert, w_exp_gate, w_exp_up, w_exp_down, norm_ple, w_ple_gate, w_ple, norm_final):
    assert w_in.shape[0] == 1 and x_prompt.shape[0] == 1 and x_sample.shape[1] == 1
    seq = x_prompt.shape[1]
    n_seq = x_sample.shape[0]
    n_pool = cache_sb_k.shape[1]
    n_pages = page_table.shape[1]
    past = n_pages * PAGE
    w_buf = state_win_k.shape[2]
    tq = 128
    assert seq % 512 == 0 and w_buf == WINDOW and past >= WINDOW and seq >= WINDOW

    scale = HEAD_DIM ** -0.5
    w = w_in[0]
    o_g = 2048 + 6 * KV_WIDTH
    w_q_sb, w_k_sb, w_v_sb, w_q_n = (w[:, i * 512:(i + 1) * 512] for i in range(4))
    w_kv6 = w[:, 2048:o_g]
    w_vs, w_vw = w[:, 2048 + 3 * KV_WIDTH:2048 + 4 * KV_WIDTH], w[:, 2048 + 5 * KV_WIDTH:o_g]
    w_ks, w_kw = w[:, 2048 + 2 * KV_WIDTH:2048 + 3 * KV_WIDTH], w[:, 2048 + 4 * KV_WIDTH:2048 + 5 * KV_WIDTH]
    aug = lambda wv, kv: _pad_lanes(wv[:, kv * HEAD_DIM:(kv + 1) * HEAD_DIM])
    w_all = jnp.concatenate([w_q_sb * scale, w_k_sb, w_v_sb, w_q_n * scale, w_kv6,
                             aug(w_vs, 0), aug(w_vs, 1), aug(w_vw, 0), aug(w_vw, 1)], axis=1).astype(BF16)
    w_t = jnp.concatenate([w_k_sb.T, w_ks.T, w_kw.T], axis=0).astype(BF16)
    g_mix = norm_mix[0][None]
    w_gate_f = jnp.concatenate([_pad_lanes(w[:, o_g:o_g + 3 * NSA_HEADS]), w[:, o_g + 3 * NSA_HEADS:]], axis=1)
    e_g = np.zeros((LANES, 3 * NSA_WIDTH), np.float32)
    for h in range(NSA_HEADS):
        for br in range(3):
            e_g[h * 3 + br, br * NSA_WIDTH + h * HEAD_DIM:br * NSA_WIDTH + (h + 1) * HEAD_DIM] = 1.0
    w_r = jnp.concatenate([w_router_expert[0], w_router_group[0],
                           jnp.zeros((D_MODEL, LANES - N_EXPERTS - N_GROUPS), F32)], axis=1)
    mw = dict(g_mix=g_mix, w_gate=w_gate_f.astype(BF16), e_g=jnp.asarray(e_g, BF16),
              w_b_sb=w_branch_sb[0].astype(BF16), w_b_nsa=w_branch_nsa[0].astype(BF16),
              w_o=w_out[0].astype(BF16), g_ffn=norm_ffn[0][None], w_r=w_r.astype(BF16))
    ew = dict(w_eg=w_exp_gate[0].astype(BF16), w_eu=w_exp_up[0].astype(BF16), w_ed=w_exp_down[0].astype(BF16),
              g_ple=norm_ple[0][None], w_pg=w_ple_gate[0].astype(BF16), w_pe=w_ple[0].astype(BF16),
              g_fin=norm_final[None])
    cw_k = _compress_weights(cmp_pe_k[0], cmp_w1_k[0], cmp_b1_k[0], cmp_w2_k[0])
    cw_v = _compress_weights(cmp_pe_v[0], cmp_w1_v[0], cmp_b1_v[0], cmp_w2_v[0])
    cw = {k: jnp.stack([cw_k[k], cw_v[k]]) for k in cw_k}

    delta = _bias_delta(rel_bias_table)
    rows = NSA_GROUP * tq
    by_kv = lambda a: a.reshape((NSA_KV, rows) + a.shape[2:])
    d_c = (np.arange(tq)[:, None] - CMP_STRIDE * (np.arange(16)[None] - 9) - (CMP_BLOCK - 1)).reshape(-1)
    l_tab = by_kv(_onehot_rows(delta, d_c).reshape(NSA_HEADS, tq, 16))
    l_hi = l_tab.astype(BF16)
    l_lo = (l_tab - l_hi.astype(F32)).astype(BF16)
    l_tab = jnp.concatenate([l_hi, l_lo, jnp.zeros((NSA_KV, rows, HEAD_DIM - 32), BF16)], axis=-1)
    d_n = 2 * tq - 1 - np.arange(3 * tq - 1)
    f_n = _onehot_rows(delta, d_n) + jnp.asarray(np.where(d_n < 0, NEG, 0.0), F32)[None]
    p_near = by_kv(_toeplitz(f_n, tq, 2 * tq))
    d_w = WINDOW + tq - 1 - np.arange(WINDOW + 2 * tq - 1)
    f_w = _onehot_rows(delta, d_w) + jnp.asarray(np.where((d_w < 0) | (d_w >= WINDOW), NEG, 0.0), F32)[None]
    p_win = by_kv(_toeplitz(f_w, tq, WINDOW + tq))

    xp = x_prompt[0]
    (q_sb_h, k_sb, v_sb, v_sb_b, q_n_h, k_c, v_c, k_s, v_s, k_w, v_w, vs_a, vw_a,
     k_sb_t, k_s_t, k_w_t) = _proj(xp, g_mix, w_all, w_t, 512)
    o_sb = _sb_prompt(q_sb_h, k_sb_t, v_sb_b, 256)

    x2 = jnp.stack([k_c, v_c]).reshape(2, seq // CMP_STRIDE, CMP_STRIDE * KV_WIDTH)
    cmp_t, cmp_a = _compress_prompt(x2, cw)
    n_cp = seq // CMP_STRIDE
    n_blk = seq // SEL_BLOCK
    c_start = np.arange(n_cp)[:, None] * CMP_STRIDE
    b_start = np.arange(n_blk)[None] * SEL_BLOCK
    overlap = ((c_start < b_start + SEL_BLOCK) & (c_start + CMP_BLOCK > b_start)
               & (np.arange(n_cp)[:, None] < n_cp - 1))
    overlap = jnp.asarray(overlap.astype(np.float32), BF16)
    o_c, sel = _nsa_cmp_prompt(q_n_h, l_tab, cmp_t[0].reshape(NSA_KV, HEAD_DIM, n_cp), cmp_a[1], overlap, tq)

    front = lambda a, n, axis: jnp.pad(a, [(n, 0) if ax == axis else (0, 0) for ax in range(a.ndim)])
    o_s = _nsa_sel_prompt(q_n_h, front(k_s_t.reshape(NSA_KV, HEAD_DIM, seq), SEL_PAD, 2),
                          front(vs_a, SEL_PAD, 1), sel, p_near, tq)
    o_w = _nsa_win_prompt(q_n_h, front(k_w_t.reshape(NSA_KV, HEAD_DIM, seq), WIN_PAD, 2),
                          front(vw_a, WIN_PAD, 1), p_win, tq)
    h_p, hn_p, comb_p = _merge(xp, o_sb, o_c, o_s, o_w, mw, 512)
    y_prompt = _moe(h_p, hn_p, comb_p, p_prompt[0, 0], ew, min(1024, seq))[None]

    w_s = jnp.concatenate([w_q_sb * scale, w_k_sb, w_v_sb, w_q_n * scale, w_kv6], axis=1).astype(BF16)
    xs = x_sample[:, 0]
    u = _proj_s(xs, g_mix, w_s)
    q_sb_s, k_sb_s, v_sb_s, q_n_s = (u[:, i * 512:(i + 1) * 512] for i in range(4))
    kc_s, vc_s, ks_s, vs_s, kw_s, vw_s = (u[:, 2048 + i * KV_WIDTH:2048 + (i + 1) * KV_WIDTH] for i in range(6))
    q_g = q_n_s.reshape(n_seq, NSA_KV, NSA_GROUP, HEAD_DIM)
    q_z = jnp.stack([jnp.pad(q_g[:, kv], ((0, 0), (kv * NSA_GROUP, 16 - (kv + 1) * NSA_GROUP), (0, 0)))
                     for kv in range(NSA_KV)], axis=1).astype(BF16)

    n_c = (past + 1 - CMP_BLOCK) // CMP_STRIDE + 1
    n_sel = -(-(past + 1) // SEL_BLOCK)
    m_rows = past // CMP_STRIDE
    n_blk_pad = -(-n_sel // LANES) * LANES
    tri_s = np.tril(np.ones((PAGE, PAGE), np.float32), -1)
    ov = np.zeros((m_rows, n_blk_pad), np.float32)
    cs = np.arange(n_c) * CMP_STRIDE
    for jb in range(n_sel):
        ov[:n_c, jb] = (cs < (jb + 1) * SEL_BLOCK) & (cs + CMP_BLOCK > jb * SEL_BLOCK)
    expand = jnp.arange(n_blk_pad)[:, None] == jnp.arange(past)[None, :] // SEL_BLOCK
    b_cmp = _onehot_rows(delta, past - (np.arange(m_rows) * CMP_STRIDE + CMP_BLOCK - 1))
    b_sel = jnp.concatenate([jnp.zeros((NSA_HEADS, past - PAGE), F32),
                             _onehot_rows(delta, PAGE - np.arange(PAGE))], axis=1)
    b_win = _onehot_rows(delta, w_buf - np.arange(w_buf))
    b_new = _onehot_rows(delta, np.zeros((1,), np.int64))
    half = CMP_BLOCK // 2
    cmp_p = ((cmp_pe_k[0], cmp_w1_k[0], cmp_b1_k[0], cmp_w2_k[0]), (cmp_pe_v[0], cmp_w1_v[0], cmp_b1_v[0], cmp_w2_v[0]))
    stack = lambda f, dt=F32: jnp.stack([f(*c) for c in cmp_p]).astype(dt)
    consts = dict(tri=jnp.asarray(tri_s),
                  pe=stack(lambda pe, w1, b1, w2: pe.reshape(2, half * HEAD_DIM)),
                  w1a=stack(lambda pe, w1, b1, w2: w1[:half * HEAD_DIM], BF16),
                  w1b=stack(lambda pe, w1, b1, w2: w1[half * HEAD_DIM:], BF16),
                  b1=stack(lambda pe, w1, b1, w2: b1[None]), w2=stack(lambda pe, w1, b1, w2: w2, BF16),
                  b_cmp=b_cmp, b_sel=b_sel, b_win=b_win, b_new=b_new,
                  ov=jnp.asarray(ov), expand=jnp.asarray(expand, BF16))

    native = lambda c: jnp.transpose(c[0], (0, 2, 3, 1))
    caches = [native(c) for c in (cache_sb_k, cache_sb_v, cache_cmp_k, cache_cmp_v, cache_sel_k, cache_sel_v)]
    new_rows = [a.reshape(n_seq, NSA_KV, HEAD_DIM) for a in (ks_s, vs_s, kw_s, vw_s)]
    o4 = _decode(page_table.reshape(-1), q_sb_s.reshape(n_seq, SB_HEADS, HEAD_DIM, 1),
                 q_n_s.reshape(n_seq, NSA_HEADS, HEAD_DIM), q_z, caches, new_rows,
                 native(state_win_k), native(state_win_v), consts, n_c, n_sel)
    o_sb_s, o_c_s, o_s_s, o_w_s = (o.reshape(n_seq, SB_WIDTH) for o in o4)
    h_s, hn_s, comb_s = _merge(xs, o_sb_s, o_c_s, o_s_s, o_w_s, mw, n_seq)
    y_sample = _moe(h_s, hn_s, comb_s, p_sample[0, :, 0], ew, n_seq)[:, None]

    pr = lambda a, n: a.reshape(1, 1, seq, n, HEAD_DIM)
    sm = lambda a, n: a.reshape(1, n_seq, 1, n, HEAD_DIM)
    win_p = lambda a: a[seq - WINDOW:].reshape(1, 1, WINDOW, NSA_KV, HEAD_DIM)
    win_s = lambda st, new: jnp.concatenate([st[0], new.reshape(n_seq, 1, NSA_KV, HEAD_DIM)], axis=1)[None, :, 1:]
    return (y_prompt, y_sample,
            pr(k_sb, SB_HEADS), pr(v_sb, SB_HEADS), pr(k_c, NSA_KV), pr(v_c, NSA_KV), pr(k_s, NSA_KV),
            pr(v_s, NSA_KV), win_p(k_w), win_p(v_w),
            sm(k_sb_s, SB_HEADS), sm(v_sb_s, SB_HEADS), sm(kc_s, NSA_KV), sm(vc_s, NSA_KV), sm(ks_s, NSA_KV),
            sm(vs_s, NSA_KV), win_s(state_win_k, kw_s), win_s(state_win_v, vw_s))
```
